```python
import math
import jax, jax.numpy as jnp
from jax import lax
import numpy as np

D_MODEL = 2048
BATCH = 2
SEQ = 16384
DEPTH = 1

N_MEM = 256
HEAD_DIM = 128
N_ATTN_HEADS = 8
ATTN_WIDTH = N_ATTN_HEADS * HEAD_DIM
CONV_CH = D_MODEL // 2
MIX_WIDTH = ATTN_WIDTH + CONV_CH
IN_WIDTH = 3 * ATTN_WIDTH + 2 * CONV_CH
CONV_K = 31
MOBA_BLOCK = 256
MOBA_TOPK = 3
Q_CHUNK = 64
N_BUCKETS = 32
MAX_DISTANCE = 2048
N_CROSS_HEADS = 4
CROSS_HEAD_DIM = 128
CROSS_WIDTH = N_CROSS_HEADS * CROSS_HEAD_DIM
D_FF = 4 * D_MODEL
EPS = 1e-6

kernel_name = "hybrid_moba_conformer_xattn_block"


def rmsnorm(x, g):
    xf = x.astype(jnp.float32)
    y = xf * lax.rsqrt(jnp.mean(xf * xf, axis=-1, keepdims=True) + EPS)
    return (y * g.astype(jnp.float32)).astype(x.dtype)


def layernorm(x, g, b):
    xf = x.astype(jnp.float32)
    mu = jnp.mean(xf, axis=-1, keepdims=True)
    xc = xf - mu
    var = jnp.mean(xc * xc, axis=-1, keepdims=True)
    y = xc * lax.rsqrt(var + EPS) * g.astype(jnp.float32) + b.astype(jnp.float32)
    return y.astype(x.dtype)


def t5_bucket(dist):
    max_exact = N_BUCKETS // 2
    nf = jnp.maximum(dist, max_exact).astype(jnp.float32)
    large = max_exact + (jnp.log(nf / max_exact) / math.log(MAX_DISTANCE / max_exact)
                         * (N_BUCKETS - max_exact)).astype(jnp.int32)
    large = jnp.minimum(large, N_BUCKETS - 1)
    return jnp.where(dist < max_exact, dist, large)


def moba_attention(q, k, v, dist_bias):
    B, H, S, Dh = q.shape
    nb = -(-S // MOBA_BLOCK)
    pad = nb * MOBA_BLOCK - S
    kp = jnp.pad(k, ((0, 0), (0, 0), (0, pad), (0, 0)))
    vp = jnp.pad(v, ((0, 0), (0, 0), (0, pad), (0, 0)))
    kb = kp.reshape(B, H, nb, MOBA_BLOCK, Dh)
    vb = vp.reshape(B, H, nb, MOBA_BLOCK, Dh)
    kbar = jnp.mean(kb.astype(jnp.float32), axis=3)
    k_eff = min(MOBA_TOPK, nb)
    scale = HEAD_DIM ** -0.5
    bi = jnp.arange(B)[:, None, None, None]
    hi = jnp.arange(H)[None, :, None, None]
    blk_offs = jnp.arange(MOBA_BLOCK)

    def chunk(ci):
        start = ci * Q_CHUNK
        qc = lax.dynamic_slice_in_dim(q, start, Q_CHUNK, axis=2)
        own = start // MOBA_BLOCK
        qpos = start + jnp.arange(Q_CHUNK)
        gate = jnp.einsum('bhqd,bhnd->bhqn', qc.astype(jnp.float32), kbar)
        gate = jnp.where(jnp.arange(nb) < own, gate, -jnp.inf)
        _, idx = lax.top_k(gate, k_eff)
        valid = jnp.arange(k_eff) < own
        ks = kb[bi, hi, idx]
        vs = vb[bi, hi, idx]
        kpos = idx[..., None] * MOBA_BLOCK + blk_offs
        d_sel = jnp.maximum(qpos[:, None, None] - kpos, 0)
        s_sel = (jnp.einsum('bhqd,bhqjkd->bhqjk', qc, ks).astype(jnp.float32) * scale
                 + dist_bias[hi[..., None], d_sel])
        s_sel = jnp.where(valid[:, None], s_sel, -jnp.inf)
        s_sel = s_sel.reshape(B, H, Q_CHUNK, k_eff * MOBA_BLOCK)
        ko = lax.dynamic_slice_in_dim(kp, own * MOBA_BLOCK, MOBA_BLOCK, axis=2)
        vo = lax.dynamic_slice_in_dim(vp, own * MOBA_BLOCK, MOBA_BLOCK, axis=2)
        d_own = qpos[:, None] - (own * MOBA_BLOCK + blk_offs)[None, :]
        s_own = (jnp.einsum('bhqd,bhkd->bhqk', qc, ko).astype(jnp.float32) * scale
                 + dist_bias[:, jnp.maximum(d_own, 0)][None])
        s_own = jnp.where(d_own >= 0, s_own, -jnp.inf)
        p = jax.nn.softmax(jnp.concatenate([s_sel, s_own], axis=-1), axis=-1)
        p_sel = p[..., :k_eff * MOBA_BLOCK].reshape(B, H, Q_CHUNK, k_eff, MOBA_BLOCK)
        p_own = p[..., k_eff * MOBA_BLOCK:]
        out = (jnp.einsum('bhqjk,bhqjkd->bhqd', p_sel.astype(vs.dtype), vs)
               + jnp.einsum('bhqk,bhkd->bhqd', p_own.astype(vo.dtype), vo))
        return out.astype(q.dtype)

    outs = lax.map(chunk, jnp.arange(S // Q_CHUNK))
    return outs.transpose(1, 0, 3, 2, 4).reshape(B, S, H * Dh)


def conformer_conv(u, conv_w, conv_b, ln_g, ln_b):
    val, gt = jnp.split(u, 2, axis=-1)
    h = val * jax.nn.sigmoid(gt)
    h = lax.conv_general_dilated(h, conv_w[:, None, :].astype(h.dtype), window_strides=(1,),
                                 padding=[(CONV_K - 1, 0)],
                                 dimension_numbers=('NWC', 'WIO', 'NWC'),
                                 feature_group_count=CONV_CH) + conv_b
    h = layernorm(h, ln_g, ln_b)
    return jax.nn.silu(h)


def cross_attention(n, m, wq, wk, wv, wo):
    B, S, _ = n.shape
    M = m.shape[1]
    q = (n @ wq).reshape(B, S, N_CROSS_HEADS, CROSS_HEAD_DIM)
    k = (m @ wk).reshape(B, M, N_CROSS_HEADS, CROSS_HEAD_DIM)
    v = (m @ wv).reshape(B, M, N_CROSS_HEADS, CROSS_HEAD_DIM)
    s = jnp.einsum('bshd,bmhd->bhsm', q, k).astype(jnp.float32) * CROSS_HEAD_DIM ** -0.5
    p = jax.nn.softmax(s, axis=-1).astype(v.dtype)
    o = jnp.einsum('bhsm,bmhd->bshd', p, v).reshape(B, S, CROSS_WIDTH)
    return o @ wo


def setup_inputs(seed: int = 0) -> dict:
    key = jax.random.key(seed)
    ks = jax.random.split(key, 24)
    f32 = jnp.float32
    nrm = lambda k, shape, s: jax.random.normal(k, shape, f32) * s
    gain = lambda k, shape: 1.0 + 0.02 * jax.random.normal(k, shape, f32)
    L = DEPTH
    return {
        "x": jax.random.normal(ks[0], (BATCH, SEQ, D_MODEL), f32),
        "mem": jax.random.normal(ks[1], (BATCH, N_MEM, D_MODEL), f32),
        "g_mix": gain(ks[2], (L, D_MODEL)),
        "w_in": nrm(ks[3], (L, D_MODEL, IN_WIDTH), D_MODEL ** -0.5),
        "conv_w": nrm(ks[4], (L, CONV_K, CONV_CH), CONV_K ** -0.5),
        "conv_b": nrm(ks[5], (L, CONV_CH), 0.02),
        "conv_ln_g": gain(ks[6], (L, CONV_CH)),
        "conv_ln_b": nrm(ks[7], (L, CONV_CH), 0.02),
        "w_out": nrm(ks[8], (L, MIX_WIDTH, D_MODEL), MIX_WIDTH ** -0.5),
        "rel_bias": nrm(ks[9], (N_BUCKETS, N_ATTN_HEADS), 0.5),
        "g_cross": gain(ks[10], (L, D_MODEL)),
        "g_mem": gain(ks[11], (L, D_MODEL)),
        "wq_c": nrm(ks[12], (L, D_MODEL, CROSS_WIDTH), D_MODEL ** -0.5),
        "wk_c": nrm(ks[13], (L, D_MODEL, CROSS_WIDTH), D_MODEL ** -0.5),
        "wv_c": nrm(ks[14], (L, D_MODEL, CROSS_WIDTH), D_MODEL ** -0.5),
        "wo_c": nrm(ks[15], (L, CROSS_WIDTH, D_MODEL), CROSS_WIDTH ** -0.5),
        "g_mlp": gain(ks[16], (L, D_MODEL)),
        "w1": nrm(ks[17], (L, D_MODEL, D_FF), D_MODEL ** -0.5),
        "w2": nrm(ks[18], (L, D_FF, D_MODEL), D_FF ** -0.5),
        "g_final": gain(ks[19], (D_MODEL,)),
    }


def reference(x, mem, g_mix, w_in, conv_w, conv_b, conv_ln_g, conv_ln_b, w_out, rel_bias,
              g_cross, g_mem, wq_c, wk_c, wv_c, wo_c, g_mlp, w1, w2, g_final):
    B, S, _ = x.shape
    dist_bias = rel_bias[t5_bucket(jnp.arange(S))].T.astype(jnp.float32)

    def to_heads(t):
        return t.reshape(B, S, N_ATTN_HEADS, HEAD_DIM).transpose(0, 2, 1, 3)

    h = x
    for l in range(DEPTH):
        n = rmsnorm(h, g_mix[l])
        z = n @ w_in[l]
        q, k, v, u = jnp.split(z, [ATTN_WIDTH, 2 * ATTN_WIDTH, 3 * ATTN_WIDTH], axis=-1)
        a = moba_attention(to_heads(q), to_heads(k), to_heads(v), dist_bias)
        c = conformer_conv(u, conv_w[l], conv_b[l], conv_ln_g[l], conv_ln_b[l])
        h = h + jnp.concatenate([a, c], axis=-1) @ w_out[l]
        h = h + cross_attention(rmsnorm(h, g_cross[l]), rmsnorm(mem, g_mem[l]),
                                wq_c[l], wk_c[l], wv_c[l], wo_c[l])
        nm = rmsnorm(h, g_mlp[l])
        h = h + jnp.square(jax.nn.relu(nm @ w1[l])) @ w2[l]
    return rmsnorm(h, g_final)
```

```python
import functools
import math

import jax
import jax.numpy as jnp
from jax import lax
from jax.experimental import pallas as pl
from jax.experimental.pallas import tpu as pltpu

D_MODEL = 2048
HEAD_DIM = 128
N_ATTN_HEADS = 8
ATTN_WIDTH = N_ATTN_HEADS * HEAD_DIM
CONV_CH = D_MODEL // 2
CONV_K = 31
MOBA_BLOCK = 256
MOBA_TOPK = 3
N_BUCKETS = 32
MAX_DISTANCE = 2048
N_CROSS_HEADS = 4
CROSS_HEAD_DIM = 128
CROSS_WIDTH = N_CROSS_HEADS * CROSS_HEAD_DIM
D_FF = 4 * D_MODEL
EPS = 1e-6

F32 = jnp.float32
BF16 = jnp.bfloat16
NEG_INF = float("-inf")

N_BIAS_TILES = 8
CONV_HALO = 32

V7X_VMEM_LIMIT = 56 * 1024 * 1024


def _params(sem):
    return pltpu.CompilerParams(dimension_semantics=sem, vmem_limit_bytes=V7X_VMEM_LIMIT)


def _rms(xf, g):
    return xf * lax.rsqrt(jnp.mean(xf * xf, axis=-1, keepdims=True) + EPS) * g


def _inproj_kernel(x_ref, g_ref, w_ref, qT_ref, k_ref, vT_ref, hg_ref, n_scr, val_scr):
    j = pl.program_id(2)

    @pl.when(j == 0)
    def _():
        n_scr[...] = _rms(x_ref[...], g_ref[...]).astype(BF16)

    z = jnp.dot(n_scr[...], w_ref[...], preferred_element_type=F32)

    @pl.when(j == 0)
    def _():
        for h in range(N_ATTN_HEADS):
            qT_ref[h] = z[:, h * HEAD_DIM:(h + 1) * HEAD_DIM].T.astype(BF16)

    @pl.when(j == 1)
    def _():
        for h in range(N_ATTN_HEADS):
            k_ref[h] = z[:, h * HEAD_DIM:(h + 1) * HEAD_DIM].astype(BF16)

    @pl.when(j == 2)
    def _():
        for h in range(N_ATTN_HEADS):
            vT_ref[h] = z[:, h * HEAD_DIM:(h + 1) * HEAD_DIM].T.astype(BF16)

    @pl.when(j == 3)
    def _():
        val_scr[...] = z

    @pl.when(j == 4)
    def _():
        hg_ref[...] = val_scr[...] * jax.nn.sigmoid(z)


def _inproj(x, g, w_bf16, tm=512):
    B, S, D = x.shape
    n_col = w_bf16.shape[1] // ATTN_WIDTH
    head_t = jax.ShapeDtypeStruct((B, N_ATTN_HEADS, HEAD_DIM, S), BF16)
    head_n = jax.ShapeDtypeStruct((B, N_ATTN_HEADS, S, HEAD_DIM), BF16)
    hg = jax.ShapeDtypeStruct((B, S, CONV_CH), F32)
    t_spec = pl.BlockSpec((None, N_ATTN_HEADS, HEAD_DIM, tm), lambda b, i, j: (b, 0, 0, i))
    n_spec = pl.BlockSpec((None, N_ATTN_HEADS, tm, HEAD_DIM), lambda b, i, j: (b, 0, i, 0))
    return pl.pallas_call(
        _inproj_kernel,
        grid=(B, S // tm, n_col),
        in_specs=[
            pl.BlockSpec((None, tm, D), lambda b, i, j: (b, i, 0)),
            pl.BlockSpec((1, D), lambda b, i, j: (0, 0)),
            pl.BlockSpec((D, ATTN_WIDTH), lambda b, i, j: (0, j)),
        ],
        out_specs=[t_spec, n_spec, t_spec,
                   pl.BlockSpec((None, tm, CONV_CH), lambda b, i, j: (b, i, 0))],
        out_shape=[head_t, head_n, head_t, hg],
        scratch_shapes=[pltpu.VMEM((tm, D), BF16), pltpu.VMEM((tm, ATTN_WIDTH), F32)],
        compiler_params=_params(("arbitrary", "arbitrary", "arbitrary")),
        name="inproj",
    )(x, g, w_bf16)


def _bias_kernel(rb_ref, out_ref):
    h = pl.program_id(0)
    koff = lax.broadcasted_iota(jnp.int32, (MOBA_BLOCK, MOBA_BLOCK), 0)
    qoff = lax.broadcasted_iota(jnp.int32, (MOBA_BLOCK, MOBA_BLOCK), 1)
    max_exact = N_BUCKETS // 2
    for delta in range(N_BIAS_TILES):
        dist = jnp.maximum(delta * MOBA_BLOCK + qoff - koff, 0)
        nf = jnp.maximum(dist, max_exact).astype(F32)
        large = max_exact + (jnp.log(nf / max_exact) / math.log(MAX_DISTANCE / max_exact)
                             * (N_BUCKETS - max_exact)).astype(jnp.int32)
        large = jnp.minimum(large, N_BUCKETS - 1)
        bucket = jnp.where(dist < max_exact, dist, large)
        val = jnp.zeros((MOBA_BLOCK, MOBA_BLOCK), F32)
        for b in range(N_BUCKETS):
            val = jnp.where(bucket == b, rb_ref[b, h], val)
        out_ref[delta] = val


def _bias_tiles(rel_bias):
    return pl.pallas_call(
        _bias_kernel,
        grid=(N_ATTN_HEADS,),
        in_specs=[pl.BlockSpec(memory_space=pltpu.SMEM)],
        out_specs=pl.BlockSpec((None, N_BIAS_TILES, MOBA_BLOCK, MOBA_BLOCK),
                               lambda h: (h, 0, 0, 0)),
        out_shape=jax.ShapeDtypeStruct(
            (N_ATTN_HEADS, N_BIAS_TILES, MOBA_BLOCK, MOBA_BLOCK), F32),
        compiler_params=_params(("arbitrary",)),
        name="bias_tiles",
    )(rel_bias)


def _moba_kernel(qT_ref, k_ref, vT_ref, bias_ref, o_ref, kbh_scr, kbl_scr, sel_scr):
    qi = pl.program_id(2)
    nb = kbh_scr.shape[0]
    scale = HEAD_DIM ** -0.5

    @pl.when(qi == 0)
    def _():
        def mean_body(n, carry):
            blk = k_ref[pl.ds(pl.multiple_of(n * MOBA_BLOCK, MOBA_BLOCK), MOBA_BLOCK), :]
            kbar = jnp.sum(blk.astype(F32), axis=0, keepdims=True) * (1.0 / MOBA_BLOCK)
            hi = kbar.astype(BF16)
            kbh_scr[pl.ds(n, 1), :] = hi.astype(F32)
            kbl_scr[pl.ds(n, 1), :] = kbar - hi.astype(F32)
            return carry
        lax.fori_loop(0, nb, mean_body, 0)

    qT = qT_ref[...]

    gate = (jnp.dot(kbh_scr[...].astype(BF16), qT, preferred_element_type=F32)
            + jnp.dot(kbl_scr[...].astype(BF16), qT, preferred_element_type=F32))
    blk_id = lax.broadcasted_iota(jnp.int32, gate.shape, 0)
    gate = jnp.where(blk_id < qi, gate, NEG_INF)
    selected = jnp.zeros(gate.shape, jnp.bool_)
    for _ in range(MOBA_TOPK):
        mx = jnp.max(gate, axis=0, keepdims=True)
        is_max = jnp.logical_and(gate == mx, mx > NEG_INF)
        first = jnp.min(jnp.where(is_max, blk_id, nb), axis=0, keepdims=True)
        pick = blk_id == first
        selected = jnp.logical_or(selected, pick)
        gate = jnp.where(pick, NEG_INF, gate)
    sel_scr[...] = jnp.where(selected, 0.0, NEG_INF)

    q0 = pl.multiple_of(qi * MOBA_BLOCK, MOBA_BLOCK)
    koff = lax.broadcasted_iota(jnp.int32, (MOBA_BLOCK, MOBA_BLOCK), 0)
    qoff = lax.broadcasted_iota(jnp.int32, (MOBA_BLOCK, MOBA_BLOCK), 1)
    s = jnp.dot(k_ref[pl.ds(q0, MOBA_BLOCK), :], qT, preferred_element_type=F32) * scale
    s = jnp.where(koff <= qoff, s + bias_ref[0], NEG_INF)
    m = jnp.max(s, axis=0, keepdims=True)
    p = jnp.exp(s - m)
    l = jnp.sum(p, axis=0, keepdims=True)
    acc = jnp.dot(vT_ref[:, pl.ds(q0, MOBA_BLOCK)], p.astype(BF16), preferred_element_type=F32)

    def body(j, carry):
        m, l, acc = carry
        k0 = pl.multiple_of(j * MOBA_BLOCK, MOBA_BLOCK)
        s = jnp.dot(k_ref[pl.ds(k0, MOBA_BLOCK), :], qT, preferred_element_type=F32) * scale
        s = s + bias_ref[jnp.minimum(qi - j, N_BIAS_TILES - 1)] + sel_scr[pl.ds(j, 1), :]
        m_new = jnp.maximum(m, jnp.max(s, axis=0, keepdims=True))
        alpha = jnp.exp(m - m_new)
        p = jnp.exp(s - m_new)
        l = alpha * l + jnp.sum(p, axis=0, keepdims=True)
        acc = alpha * acc + jnp.dot(vT_ref[:, pl.ds(k0, MOBA_BLOCK)], p.astype(BF16),
                                    preferred_element_type=F32)
        return m_new, l, acc

    m, l, acc = lax.fori_loop(0, qi, body, (m, l, acc))
    o_ref[...] = (acc / l).T.astype(o_ref.dtype)


def _moba(qT, k, vT, bias):
    B, H, _, S = qT.shape
    nb = S // MOBA_BLOCK
    return pl.pallas_call(
        _moba_kernel,
        grid=(B, H, nb),
        in_specs=[
            pl.BlockSpec((None, None, HEAD_DIM, MOBA_BLOCK), lambda b, h, i: (b, h, 0, i)),
            pl.BlockSpec((None, None, S, HEAD_DIM), lambda b, h, i: (b, h, 0, 0)),
            pl.BlockSpec((None, None, HEAD_DIM, S), lambda b, h, i: (b, h, 0, 0)),
            pl.BlockSpec((None, N_BIAS_TILES, MOBA_BLOCK, MOBA_BLOCK),
                         lambda b, h, i: (h, 0, 0, 0)),
        ],
        out_specs=pl.BlockSpec((None, MOBA_BLOCK, HEAD_DIM), lambda b, h, i: (b, i, h)),
        out_shape=jax.ShapeDtypeStruct((B, S, ATTN_WIDTH), BF16),
        scratch_shapes=[pltpu.VMEM((nb, HEAD_DIM), F32), pltpu.VMEM((nb, HEAD_DIM), F32),
                        pltpu.VMEM((nb, MOBA_BLOCK), F32)],
        compiler_params=_params(("arbitrary", "arbitrary", "arbitrary")),
        name="moba",
    )(qT, k, vT, bias)


def _conv_kernel(cur_ref, halo_ref, w_ref, b_ref, g_ref, beta_ref, o_ref, xs_scr, y_scr):
    i = pl.program_id(1)
    ts = cur_ref.shape[0]
    rc = 64
    halo = halo_ref[...]
    xs_scr[0:CONV_HALO, :] = jnp.where(i == 0, 0.0, halo)
    xs_scr[CONV_HALO:, :] = cur_ref[...]
    first_tap = CONV_HALO - (CONV_K - 1)
    for c in range(CONV_CH // 128):
        lanes = slice(c * 128, (c + 1) * 128)
        for r in range(ts // rc):
            acc = jnp.zeros((rc, 128), F32)
            for t in range(CONV_K):
                r0 = first_tap + t + r * rc
                acc = acc + w_ref[t:t + 1, lanes] * xs_scr[r0:r0 + rc, lanes]
            y_scr[r * rc:(r + 1) * rc, lanes] = acc + b_ref[:, lanes]
    y = y_scr[...]
    mu = jnp.mean(y, axis=-1, keepdims=True)
    yc = y - mu
    var = jnp.mean(yc * yc, axis=-1, keepdims=True)
    z = yc * lax.rsqrt(var + EPS) * g_ref[...] + beta_ref[...]
    o_ref[...] = (z * jax.nn.sigmoid(z)).astype(o_ref.dtype)


def _conv_branch(hg, conv_w, conv_b, ln_g, ln_b, ts=256):
    B, S, C = hg.shape
    halo_blocks = ts // CONV_HALO
    vec = pl.BlockSpec((1, C), lambda b, i: (0, 0))
    return pl.pallas_call(
        _conv_kernel,
        grid=(B, S // ts),
        in_specs=[
            pl.BlockSpec((None, ts, C), lambda b, i: (b, i, 0)),
            pl.BlockSpec((None, CONV_HALO, C),
                         lambda b, i: (b, jnp.maximum(i * halo_blocks - 1, 0), 0)),
            pl.BlockSpec((CONV_K, C), lambda b, i: (0, 0)),
            vec, vec, vec,
        ],
        out_specs=pl.BlockSpec((None, ts, C), lambda b, i: (b, i, 0)),
        out_shape=jax.ShapeDtypeStruct((B, S, C), BF16),
        scratch_shapes=[pltpu.VMEM((ts + CONV_HALO, C), F32), pltpu.VMEM((ts, C), F32)],
        compiler_params=_params(("arbitrary", "arbitrary")),
        name="conv_branch",
    )(hg, hg, conv_w, conv_b, ln_g, ln_b)


def _outproj_kernel(x_ref, a_ref, c_ref, wa_ref, wc_ref, o_ref):
    o_ref[...] = (x_ref[...]
                  + jnp.dot(a_ref[...], wa_ref[...], preferred_element_type=F32)
                  + jnp.dot(c_ref[...], wc_ref[...], preferred_element_type=F32))


def _outproj(x2, a2, c2, w_out_bf16, tm=512, tn=1024):
    M, D = x2.shape
    return pl.pallas_call(
        _outproj_kernel,
        grid=(M // tm, D // tn),
        in_specs=[
            pl.BlockSpec((tm, tn), lambda i, j: (i, j)),
            pl.BlockSpec((tm, ATTN_WIDTH), lambda i, j: (i, 0)),
            pl.BlockSpec((tm, CONV_CH), lambda i, j: (i, 0)),
            pl.BlockSpec((ATTN_WIDTH, tn), lambda i, j: (0, j)),
            pl.BlockSpec((CONV_CH, tn), lambda i, j: (1, j)),
        ],
        out_specs=pl.BlockSpec((tm, tn), lambda i, j: (i, j)),
        out_shape=jax.ShapeDtypeStruct((M, D), F32),
        compiler_params=_params(("arbitrary", "arbitrary")),
        name="outproj",
    )(x2, a2, c2, w_out_bf16, w_out_bf16)


def _memkv_kernel(m_ref, g_ref, wk_ref, wv_ref, k_ref, v_ref):
    n = _rms(m_ref[...], g_ref[...]).astype(BF16)
    k_ref[...] = jnp.dot(n, wk_ref[...], preferred_element_type=F32).astype(BF16)
    v_ref[...] = jnp.dot(n, wv_ref[...], preferred_element_type=F32).astype(BF16)


def _memkv(mem2, g, wk, wv):
    M, D = mem2.shape
    full = lambda shape: pl.BlockSpec(shape, lambda i: (0, 0))
    out = jax.ShapeDtypeStruct((M, CROSS_WIDTH), BF16)
    return pl.pallas_call(
        _memkv_kernel,
        grid=(1,),
        in_specs=[full((M, D)), full((1, D)), full((D, CROSS_WIDTH)), full((D, CROSS_WIDTH))],
        out_specs=[full((M, CROSS_WIDTH)), full((M, CROSS_WIDTH))],
        out_shape=[out, out],
        compiler_params=_params(("arbitrary",)),
        name="mem_kv",
    )(mem2, g, wk, wv)


def _cross_kernel(h_ref, g_ref, wq_ref, kc_ref, vc_ref, wo_ref, o_ref):
    h = h_ref[...]
    n = _rms(h, g_ref[...]).astype(BF16)
    q = jnp.dot(n, wq_ref[...], preferred_element_type=F32).astype(BF16)
    scale = CROSS_HEAD_DIM ** -0.5
    outs = []
    for hd in range(N_CROSS_HEADS):
        cols = slice(hd * CROSS_HEAD_DIM, (hd + 1) * CROSS_HEAD_DIM)
        s = lax.dot_general(q[:, cols], kc_ref[:, cols], (((1,), (1,)), ((), ())),
                            preferred_element_type=F32) * scale
        m = jnp.max(s, axis=-1, keepdims=True)
        p = jnp.exp(s - m)
        p = p / jnp.sum(p, axis=-1, keepdims=True)
        outs.append(jnp.dot(p.astype(BF16), vc_ref[:, cols], preferred_element_type=F32))
    o = jnp.concatenate(outs, axis=-1).astype(BF16)
    o_ref[...] = h + jnp.dot(o, wo_ref[...], preferred_element_type=F32)


def _cross(h1, g, wq, kc, vc, wo, tm=512):
    B, S, D = h1.shape
    n_mem = kc.shape[1]
    const = lambda shape: pl.BlockSpec(shape, lambda b, i: (0, 0))
    return pl.pallas_call(
        _cross_kernel,
        grid=(B, S // tm),
        in_specs=[
            pl.BlockSpec((None, tm, D), lambda b, i: (b, i, 0)),
            const((1, D)),
            const((D, CROSS_WIDTH)),
            pl.BlockSpec((None, n_mem, CROSS_WIDTH), lambda b, i: (b, 0, 0)),
            pl.BlockSpec((None, n_mem, CROSS_WIDTH), lambda b, i: (b, 0, 0)),
            const((CROSS_WIDTH, D)),
        ],
        out_specs=pl.BlockSpec((None, tm, D), lambda b, i: (b, i, 0)),
        out_shape=jax.ShapeDtypeStruct((B, S, D), F32),
        compiler_params=_params(("arbitrary", "arbitrary")),
        name="cross_attn",
    )(h1, g, wq, kc, vc, wo)


def _mlp_kernel(h_ref, g_ref, w1_ref, w2_ref, gf_ref, o_ref, n_scr, acc_scr):
    f = pl.program_id(1)

    @pl.when(f == 0)
    def _():
        h = h_ref[...]
        n_scr[...] = _rms(h, g_ref[...]).astype(BF16)
        acc_scr[...] = h

    u = jnp.dot(n_scr[...], w1_ref[...], preferred_element_type=F32)
    u = jnp.square(jnp.maximum(u, 0.0)).astype(BF16)
    acc_scr[...] += jnp.dot(u, w2_ref[...], preferred_element_type=F32)

    @pl.when(f == pl.num_programs(1) - 1)
    def _():
        o_ref[...] = _rms(acc_scr[...], gf_ref[...])


def _mlp(h2, g, w1, w2, g_final, tm=512, tf=1024):
    M, D = h2.shape
    vec = pl.BlockSpec((1, D), lambda i, f: (0, 0))
    return pl.pallas_call(
        _mlp_kernel,
        grid=(M // tm, D_FF // tf),
        in_specs=[
            pl.BlockSpec((tm, D), lambda i, f: (i, 0)),
            vec,
            pl.BlockSpec((D, tf), lambda i, f: (0, f)),
            pl.BlockSpec((tf, D), lambda i, f: (f, 0)),
            vec,
        ],
        out_specs=pl.BlockSpec((tm, D), lambda i, f: (i, 0)),
        out_shape=jax.ShapeDtypeStruct((M, D), F32),
        scratch_shapes=[pltpu.VMEM((tm, D), BF16), pltpu.VMEM((tm, D), F32)],
        compiler_params=_params(("arbitrary", "arbitrary")),
        name="mlp",
    )(h2, g, w1, w2, g_final)


@jax.jit
def _layer(x, mem, g_mix, w_in, conv_w, conv_b, conv_ln_g, conv_ln_b, w_out, rel_bias,
           g_cross, g_mem, wq_c, wk_c, wv_c, wo_c, g_mlp, w1, w2, g_final):
    B, S, D = x.shape
    row = lambda v: v.reshape(1, -1)
    bias = _bias_tiles(rel_bias)
    qT, k, vT, hg = _inproj(x, row(g_mix), w_in.astype(BF16))
    a = _moba(qT, k, vT, bias)
    c = _conv_branch(hg, conv_w, row(conv_b), row(conv_ln_g), row(conv_ln_b))
    h1 = _outproj(x.reshape(B * S, D), a.reshape(B * S, ATTN_WIDTH),
                  c.reshape(B * S, CONV_CH), w_out.astype(BF16))
    kc, vc = _memkv(mem.reshape(-1, D), row(g_mem), wk_c.astype(BF16), wv_c.astype(BF16))
    n_mem = mem.shape[1]
    h2 = _cross(h1.reshape(B, S, D), row(g_cross), wq_c.astype(BF16),
                kc.reshape(B, n_mem, CROSS_WIDTH), vc.reshape(B, n_mem, CROSS_WIDTH),
                wo_c.astype(BF16))
    out = _mlp(h2.reshape(B * S, D), row(g_mlp), w1.astype(BF16), w2.astype(BF16),
               row(g_final))
    return out.reshape(B, S, D)


def kernel(x, mem, g_mix, w_in, conv_w, conv_b, conv_ln_g, conv_ln_b, w_out, rel_bias,
           g_cross, g_mem, wq_c, wk_c, wv_c, wo_c, g_mlp, w1, w2, g_final):
    return _layer(x, mem, g_mix[0], w_in[0], conv_w[0], conv_b[0], conv_ln_g[0], conv_ln_b[0],
                  w_out[0], rel_bias, g_cross[0], g_mem[0], wq_c[0], wk_c[0], wv_c[0],
                  wo_c[0], g_mlp[0], w1[0], w2[0], g_final)
```

```python
import functools
import math

import jax
import jax.numpy as jnp
from jax import lax
from jax.experimental import pallas as pl
from jax.experimental.pallas import tpu as pltpu

D_MODEL = 2048
HEAD_DIM = 128
N_ATTN_HEADS = 8
ATTN_WIDTH = N_ATTN_HEADS * HEAD_DIM
CONV_CH = D_MODEL // 2
CONV_K = 31
MOBA_BLOCK = 256
MOBA_TOPK = 3
N_BUCKETS = 32
MAX_DISTANCE = 2048
N_CROSS_HEADS = 4
CROSS_HEAD_DIM = 128
CROSS_WIDTH = N_CROSS_HEADS * CROSS_HEAD_DIM
D_FF = 4 * D_MODEL
EPS = 1e-6

F32 = jnp.float32
BF16 = jnp.bfloat16
NEG_INF = float("-inf")
LOG2E = math.log2(math.e)

N_BIAS_TILES = 8
PAST_GROUP = 2
CONV_HALO = 32

V7X_VMEM_LIMIT = 56 * 1024 * 1024


def _params(sem):
    return pltpu.CompilerParams(dimension_semantics=sem, vmem_limit_bytes=V7X_VMEM_LIMIT)


def _rms(xf, g):
    return xf * lax.rsqrt(jnp.mean(xf * xf, axis=-1, keepdims=True) + EPS) * g


def _inproj_kernel(x_ref, g_ref, w_ref, qT_ref, k_ref, vT_ref, hg_ref, n_scr, val_scr):
    j = pl.program_id(2)

    @pl.when(j == 0)
    def _():
        n_scr[...] = _rms(x_ref[...], g_ref[...]).astype(BF16)

    z = jnp.dot(n_scr[...], w_ref[...], preferred_element_type=F32)

    @pl.when(j == 0)
    def _():
        for h in range(N_ATTN_HEADS):
            qT_ref[h] = z[:, h * HEAD_DIM:(h + 1) * HEAD_DIM].T.astype(BF16)

    @pl.when(j == 1)
    def _():
        for h in range(N_ATTN_HEADS):
            k_ref[h] = z[:, h * HEAD_DIM:(h + 1) * HEAD_DIM].astype(BF16)

    @pl.when(j == 2)
    def _():
        for h in range(N_ATTN_HEADS):
            vT_ref[h] = z[:, h * HEAD_DIM:(h + 1) * HEAD_DIM].T.astype(BF16)

    @pl.when(j == 3)
    def _():
        val_scr[...] = z

    @pl.when(j == 4)
    def _():
        hg_ref[...] = val_scr[...] * jax.nn.sigmoid(z)


def _inproj(x, g, w_bf16, tm=512):
    B, S, D = x.shape
    n_col = w_bf16.shape[1] // ATTN_WIDTH
    head_t = jax.ShapeDtypeStruct((B, N_ATTN_HEADS, HEAD_DIM, S), BF16)
    head_n = jax.ShapeDtypeStruct((B, N_ATTN_HEADS, S, HEAD_DIM), BF16)
    hg = jax.ShapeDtypeStruct((B, S, CONV_CH), F32)
    t_spec = pl.BlockSpec((None, N_ATTN_HEADS, HEAD_DIM, tm), lambda b, i, j: (b, 0, 0, i))
    n_spec = pl.BlockSpec((None, N_ATTN_HEADS, tm, HEAD_DIM), lambda b, i, j: (b, 0, i, 0))
    return pl.pallas_call(
        _inproj_kernel,
        grid=(B, S // tm, n_col),
        in_specs=[
            pl.BlockSpec((None, tm, D), lambda b, i, j: (b, i, 0)),
            pl.BlockSpec((1, D), lambda b, i, j: (0, 0)),
            pl.BlockSpec((D, ATTN_WIDTH), lambda b, i, j: (0, j)),
        ],
        out_specs=[t_spec, n_spec, t_spec,
                   pl.BlockSpec((None, tm, CONV_CH), lambda b, i, j: (b, i, 0))],
        out_shape=[head_t, head_n, head_t, hg],
        scratch_shapes=[pltpu.VMEM((tm, D), BF16), pltpu.VMEM((tm, ATTN_WIDTH), F32)],
        compiler_params=_params(("arbitrary", "arbitrary", "arbitrary")),
        name="inproj",
    )(x, g, w_bf16)


def _bias_kernel(rb_ref, out_ref):
    h = pl.program_id(0)
    koff = lax.broadcasted_iota(jnp.int32, (MOBA_BLOCK, MOBA_BLOCK), 0)
    qoff = lax.broadcasted_iota(jnp.int32, (MOBA_BLOCK, MOBA_BLOCK), 1)
    max_exact = N_BUCKETS // 2
    for delta in range(N_BIAS_TILES):
        dist = jnp.maximum(delta * MOBA_BLOCK + qoff - koff, 0)
        nf = jnp.maximum(dist, max_exact).astype(F32)
        large = max_exact + (jnp.log(nf / max_exact) / math.log(MAX_DISTANCE / max_exact)
                             * (N_BUCKETS - max_exact)).astype(jnp.int32)
        large = jnp.minimum(large, N_BUCKETS - 1)
        bucket = jnp.where(dist < max_exact, dist, large)
        val = jnp.zeros((MOBA_BLOCK, MOBA_BLOCK), F32)
        for b in range(N_BUCKETS):
            val = jnp.where(bucket == b, rb_ref[b, h], val)
        out_ref[delta] = val * LOG2E


def _bias_tiles(rel_bias):
    return pl.pallas_call(
        _bias_kernel,
        grid=(N_ATTN_HEADS,),
        in_specs=[pl.BlockSpec(memory_space=pltpu.SMEM)],
        out_specs=pl.BlockSpec((None, N_BIAS_TILES, MOBA_BLOCK, MOBA_BLOCK),
                               lambda h: (h, 0, 0, 0)),
        out_shape=jax.ShapeDtypeStruct(
            (N_ATTN_HEADS, N_BIAS_TILES, MOBA_BLOCK, MOBA_BLOCK), F32),
        compiler_params=_params(("arbitrary",)),
        name="bias_tiles",
    )(rel_bias)


def _moba_kernel(qT_ref, k_ref, vT_ref, bias_ref, o_ref, kbh_scr, kbl_scr, sel_scr, s_scr):
    qi = pl.program_id(2)
    nb = kbh_scr.shape[0]
    scale = HEAD_DIM ** -0.5

    @pl.when(qi == 0)
    def _():
        def mean_body(n, carry):
            blk = k_ref[pl.ds(pl.multiple_of(n * MOBA_BLOCK, MOBA_BLOCK), MOBA_BLOCK), :]
            kbar = jnp.sum(blk.astype(F32), axis=0, keepdims=True) * (1.0 / MOBA_BLOCK)
            hi = kbar.astype(BF16)
            kbh_scr[pl.ds(n, 1), :] = hi.astype(F32)
            kbl_scr[pl.ds(n, 1), :] = kbar - hi.astype(F32)
            return carry
        lax.fori_loop(0, nb, mean_body, 0)

    qT = qT_ref[...]

    gate = (jnp.dot(kbh_scr[...].astype(BF16), qT, preferred_element_type=F32)
            + jnp.dot(kbl_scr[...].astype(BF16), qT, preferred_element_type=F32))
    blk_id = lax.broadcasted_iota(jnp.int32, gate.shape, 0)
    gate = jnp.where(blk_id < qi, gate, NEG_INF)
    selected = jnp.zeros(gate.shape, jnp.bool_)
    for _ in range(MOBA_TOPK):
        mx = jnp.max(gate, axis=0, keepdims=True)
        is_max = jnp.logical_and(gate == mx, mx > NEG_INF)
        first = jnp.min(jnp.where(is_max, blk_id, nb), axis=0, keepdims=True)
        pick = blk_id == first
        selected = jnp.logical_or(selected, pick)
        gate = jnp.where(pick, NEG_INF, gate)
    sel_scr[...] = jnp.where(selected, 0.0, NEG_INF)

    c1 = scale * LOG2E

    q0 = pl.multiple_of(qi * MOBA_BLOCK, MOBA_BLOCK)
    koff = lax.broadcasted_iota(jnp.int32, (MOBA_BLOCK, MOBA_BLOCK), 0)
    qoff = lax.broadcasted_iota(jnp.int32, (MOBA_BLOCK, MOBA_BLOCK), 1)
    s = jnp.dot(k_ref[pl.ds(q0, MOBA_BLOCK), :], qT, preferred_element_type=F32) * c1
    s = jnp.where(koff <= qoff, s + bias_ref[0], NEG_INF)
    m = jnp.max(s, axis=0, keepdims=True)
    p = jnp.exp2(s - m)
    l = jnp.sum(p, axis=0, keepdims=True)
    acc = jnp.dot(vT_ref[:, pl.ds(q0, MOBA_BLOCK)], p.astype(BF16), preferred_element_type=F32)

    span = PAST_GROUP * MOBA_BLOCK
    last_group = nb // PAST_GROUP - 1

    def prep(g, slot):
        gg = jnp.minimum(g, last_group)
        k0 = pl.multiple_of(gg * span, span)
        s_all = jnp.dot(k_ref[pl.ds(k0, span), :], qT, preferred_element_type=F32) * c1
        mx = None
        for t in range(PAST_GROUP):
            j = gg * PAST_GROUP + t
            rows = slice(t * MOBA_BLOCK, (t + 1) * MOBA_BLOCK)
            x = (s_all[rows] + bias_ref[jnp.clip(qi - j, 0, N_BIAS_TILES - 1)]
                 + sel_scr[pl.ds(j, 1), :])
            s_scr[slot, rows, :] = x
            mx = x if mx is None else jnp.maximum(mx, x)
        return jnp.max(mx, axis=0, keepdims=True)

    def consume(g, slot, m, l, acc, mx):
        k0 = pl.multiple_of(jnp.minimum(g, last_group) * span, span)
        m_new = jnp.maximum(m, mx)
        alpha = jnp.exp2(m - m_new)
        p = jnp.exp2(s_scr[slot] - m_new)
        l = alpha * l + jnp.sum(p, axis=0, keepdims=True)
        acc = alpha * acc + jnp.dot(vT_ref[:, pl.ds(k0, span)], p.astype(BF16),
                                    preferred_element_type=F32)
        return m_new, l, acc

    def body(t, carry):
        m, l, acc, mx0 = carry
        g = 2 * t
        mx1 = prep(g + 1, 1)
        m, l, acc = consume(g, 0, m, l, acc, mx0)
        mx0 = prep(g + 2, 0)
        m, l, acc = consume(g + 1, 1, m, l, acc, mx1)
        return m, l, acc, mx0

    n_groups = (qi + (PAST_GROUP - 1)) // PAST_GROUP
    m, l, acc, _ = lax.fori_loop(0, (n_groups + 1) // 2, body, (m, l, acc, prep(0, 0)))
    o_ref[...] = (acc / l).T.astype(o_ref.dtype)


def _moba(qT, k, vT, bias):
    B, H, _, S = qT.shape
    nb = S // MOBA_BLOCK
    return pl.pallas_call(
        _moba_kernel,
        grid=(B, H, nb),
        in_specs=[
            pl.BlockSpec((None, None, HEAD_DIM, MOBA_BLOCK), lambda b, h, i: (b, h, 0, i)),
            pl.BlockSpec((None, None, S, HEAD_DIM), lambda b, h, i: (b, h, 0, 0)),
            pl.BlockSpec((None, None, HEAD_DIM, S), lambda b, h, i: (b, h, 0, 0)),
            pl.BlockSpec((None, N_BIAS_TILES, MOBA_BLOCK, MOBA_BLOCK),
                         lambda b, h, i: (h, 0, 0, 0)),
        ],
        out_specs=pl.BlockSpec((None, MOBA_BLOCK, HEAD_DIM), lambda b, h, i: (b, i, h)),
        out_shape=jax.ShapeDtypeStruct((B, S, ATTN_WIDTH), BF16),
        scratch_shapes=[pltpu.VMEM((nb, HEAD_DIM), F32), pltpu.VMEM((nb, HEAD_DIM), F32),
                        pltpu.VMEM((nb, MOBA_BLOCK), F32),
                        pltpu.VMEM((2, PAST_GROUP * MOBA_BLOCK, MOBA_BLOCK), F32)],
        compiler_params=_params(("arbitrary", "arbitrary", "arbitrary")),
        name="moba",
    )(qT, k, vT, bias)


def _conv_kernel(cur_ref, halo_ref, w_ref, b_ref, g_ref, beta_ref, o_ref, xs_scr, y_scr):
    i = pl.program_id(1)
    ts = cur_ref.shape[0]
    rc = 64
    halo = halo_ref[...]
    xs_scr[0:CONV_HALO, :] = jnp.where(i == 0, 0.0, halo)
    xs_scr[CONV_HALO:, :] = cur_ref[...]
    first_tap = CONV_HALO - (CONV_K - 1)
    for c in range(CONV_CH // 128):
        lanes = slice(c * 128, (c + 1) * 128)
        for r in range(ts // rc):
            acc = jnp.zeros((rc, 128), F32)
            for t in range(CONV_K):
                r0 = first_tap + t + r * rc
                acc = acc + w_ref[t:t + 1, lanes] * xs_scr[r0:r0 + rc, lanes]
            y_scr[r * rc:(r + 1) * rc, lanes] = acc + b_ref[:, lanes]
    y = y_scr[...]
    mu = jnp.mean(y, axis=-1, keepdims=True)
    yc = y - mu
    var = jnp.mean(yc * yc, axis=-1, keepdims=True)
    z = yc * lax.rsqrt(var + EPS) * g_ref[...] + beta_ref[...]
    o_ref[...] = (z * jax.nn.sigmoid(z)).astype(o_ref.dtype)


def _conv_branch(hg, conv_w, conv_b, ln_g, ln_b, ts=256):
    B, S, C = hg.shape
    halo_blocks = ts // CONV_HALO
    vec = pl.BlockSpec((1, C), lambda b, i: (0, 0))
    return pl.pallas_call(
        _conv_kernel,
        grid=(B, S // ts),
        in_specs=[
            pl.BlockSpec((None, ts, C), lambda b, i: (b, i, 0)),
            pl.BlockSpec((None, CONV_HALO, C),
                         lambda b, i: (b, jnp.maximum(i * halo_blocks - 1, 0), 0)),
            pl.BlockSpec((CONV_K, C), lambda b, i: (0, 0)),
            vec, vec, vec,
        ],
        out_specs=pl.BlockSpec((None, ts, C), lambda b, i: (b, i, 0)),
        out_shape=jax.ShapeDtypeStruct((B, S, C), BF16),
        scratch_shapes=[pltpu.VMEM((ts + CONV_HALO, C), F32), pltpu.VMEM((ts, C), F32)],
        compiler_params=_params(("arbitrary", "arbitrary")),
        name="conv_branch",
    )(hg, hg, conv_w, conv_b, ln_g, ln_b)


def _outproj_kernel(x_ref, a_ref, c_ref, wa_ref, wc_ref, o_ref):
    o_ref[...] = (x_ref[...]
                  + jnp.dot(a_ref[...], wa_ref[...], preferred_element_type=F32)
                  + jnp.dot(c_ref[...], wc_ref[...], preferred_element_type=F32))


def _outproj(x2, a2, c2, w_out_bf16, tm=512, tn=1024):
    M, D = x2.shape
    return pl.pallas_call(
        _outproj_kernel,
        grid=(M // tm, D // tn),
        in_specs=[
            pl.BlockSpec((tm, tn), lambda i, j: (i, j)),
            pl.BlockSpec((tm, ATTN_WIDTH), lambda i, j: (i, 0)),
            pl.BlockSpec((tm, CONV_CH), lambda i, j: (i, 0)),
            pl.BlockSpec((ATTN_WIDTH, tn), lambda i, j: (0, j)),
            pl.BlockSpec((CONV_CH, tn), lambda i, j: (1, j)),
        ],
        out_specs=pl.BlockSpec((tm, tn), lambda i, j: (i, j)),
        out_shape=jax.ShapeDtypeStruct((M, D), F32),
        compiler_params=_params(("arbitrary", "arbitrary")),
        name="outproj",
    )(x2, a2, c2, w_out_bf16, w_out_bf16)


def _memkv_kernel(m_ref, g_ref, wk_ref, wv_ref, k_ref, v_ref):
    n = _rms(m_ref[...], g_ref[...]).astype(BF16)
    k_ref[...] = jnp.dot(n, wk_ref[...], preferred_element_type=F32).astype(BF16)
    v_ref[...] = jnp.dot(n, wv_ref[...], preferred_element_type=F32).astype(BF16)


def _memkv(mem2, g, wk, wv):
    M, D = mem2.shape
    full = lambda shape: pl.BlockSpec(shape, lambda i: (0, 0))
    out = jax.ShapeDtypeStruct((M, CROSS_WIDTH), BF16)
    return pl.pallas_call(
        _memkv_kernel,
        grid=(1,),
        in_specs=[full((M, D)), full((1, D)), full((D, CROSS_WIDTH)), full((D, CROSS_WIDTH))],
        out_specs=[full((M, CROSS_WIDTH)), full((M, CROSS_WIDTH))],
        out_shape=[out, out],
        compiler_params=_params(("arbitrary",)),
        name="mem_kv",
    )(mem2, g, wk, wv)


def _cross_kernel(h_ref, g_ref, wq_ref, kc_ref, vc_ref, wo_ref, o_ref):
    h = h_ref[...]
    n = _rms(h, g_ref[...]).astype(BF16)
    q = jnp.dot(n, wq_ref[...], preferred_element_type=F32).astype(BF16)
    scale = CROSS_HEAD_DIM ** -0.5
    outs = []
    for hd in range(N_CROSS_HEADS):
        cols = slice(hd * CROSS_HEAD_DIM, (hd + 1) * CROSS_HEAD_DIM)
        s = lax.dot_general(q[:, cols], kc_ref[:, cols], (((1,), (1,)), ((), ())),
                            preferred_element_type=F32) * scale
        m = jnp.max(s, axis=-1, keepdims=True)
        p = jnp.exp(s - m)
        p = p / jnp.sum(p, axis=-1, keepdims=True)
        outs.append(jnp.dot(p.astype(BF16), vc_ref[:, cols], preferred_element_type=F32))
    o = jnp.concatenate(outs, axis=-1).astype(BF16)
    o_ref[...] = h + jnp.dot(o, wo_ref[...], preferred_element_type=F32)


def _cross(h1, g, wq, kc, vc, wo, tm=512):
    B, S, D = h1.shape
    n_mem = kc.shape[1]
    const = lambda shape: pl.BlockSpec(shape, lambda b, i: (0, 0))
    return pl.pallas_call(
        _cross_kernel,
        grid=(B, S // tm),
        in_specs=[
            pl.BlockSpec((None, tm, D), lambda b, i: (b, i, 0)),
            const((1, D)),
            const((D, CROSS_WIDTH)),
            pl.BlockSpec((None, n_mem, CROSS_WIDTH), lambda b, i: (b, 0, 0)),
            pl.BlockSpec((None, n_mem, CROSS_WIDTH), lambda b, i: (b, 0, 0)),
            const((CROSS_WIDTH, D)),
        ],
        out_specs=pl.BlockSpec((None, tm, D), lambda b, i: (b, i, 0)),
        out_shape=jax.ShapeDtypeStruct((B, S, D), F32),
        compiler_params=_params(("arbitrary", "arbitrary")),
        name="cross_attn",
    )(h1, g, wq, kc, vc, wo)


def _mlp_kernel(h_ref, g_ref, w1_ref, w2_ref, gf_ref, o_ref, n_scr, acc_scr):
    f = pl.program_id(1)

    @pl.when(f == 0)
    def _():
        h = h_ref[...]
        n_scr[...] = _rms(h, g_ref[...]).astype(BF16)
        acc_scr[...] = h

    u = jnp.dot(n_scr[...], w1_ref[...], preferred_element_type=F32)
    u = jnp.square(jnp.maximum(u, 0.0)).astype(BF16)
    acc_scr[...] += jnp.dot(u, w2_ref[...], preferred_element_type=F32)

    @pl.when(f == pl.num_programs(1) - 1)
    def _():
        o_ref[...] = _rms(acc_scr[...], gf_ref[...])


def _mlp(h2, g, w1, w2, g_final, tm=512, tf=1024):
    M, D = h2.shape
    vec = pl.BlockSpec((1, D), lambda i, f: (0, 0))
    return pl.pallas_call(
        _mlp_kernel,
        grid=(M // tm, D_FF // tf),
        in_specs=[
            pl.BlockSpec((tm, D), lambda i, f: (i, 0)),
            vec,
            pl.BlockSpec((D, tf), lambda i, f: (0, f)),
            pl.BlockSpec((tf, D), lambda i, f: (f, 0)),
            vec,
        ],
        out_specs=pl.BlockSpec((tm, D), lambda i, f: (i, 0)),
        out_shape=jax.ShapeDtypeStruct((M, D), F32),
        scratch_shapes=[pltpu.VMEM((tm, D), BF16), pltpu.VMEM((tm, D), F32)],
        compiler_params=_params(("arbitrary", "arbitrary")),
        name="mlp",
    )(h2, g, w1, w2, g_final)


@jax.jit
def _layer(x, mem, g_mix, w_in, conv_w, conv_b, conv_ln_g, conv_ln_b, w_out, rel_bias,
           g_cross, g_mem, wq_c, wk_c, wv_c, wo_c, g_mlp, w1, w2, g_final):
    B, S, D = x.shape
    row = lambda v: v.reshape(1, -1)
    bias = _bias_tiles(rel_bias)
    qT, k, vT, hg = _inproj(x, row(g_mix), w_in.astype(BF16))
    a = _moba(qT, k, vT, bias)
    c = _conv_branch(hg, conv_w, row(conv_b), row(conv_ln_g), row(conv_ln_b))
    h1 = _outproj(x.reshape(B * S, D), a.reshape(B * S, ATTN_WIDTH),
                  c.reshape(B * S, CONV_CH), w_out.astype(BF16))
    kc, vc = _memkv(mem.reshape(-1, D), row(g_mem), wk_c.astype(BF16), wv_c.astype(BF16))
    n_mem = mem.shape[1]
    h2 = _cross(h1.reshape(B, S, D), row(g_cross), wq_c.astype(BF16),
                kc.reshape(B, n_mem, CROSS_WIDTH), vc.reshape(B, n_mem, CROSS_WIDTH),
                wo_c.astype(BF16))
    out = _mlp(h2.reshape(B * S, D), row(g_mlp), w1.astype(BF16), w2.astype(BF16),
               row(g_final))
    return out.reshape(B, S, D)


def kernel(x, mem, g_mix, w_in, conv_w, conv_b, conv_ln_g, conv_ln_b, w_out, rel_bias,
           g_cross, g_mem, wq_c, wk_c, wv_c, wo_c, g_mlp, w1, w2, g_final):
    return _layer(x, mem, g_mix[0], w_in[0], conv_w[0], conv_b[0], conv_ln_g[0], conv_ln_b[0],
                  w_out[0], rel_bias, g_cross[0], g_mem[0], wq_c[0], wk_c[0], wv_c[0],
                  wo_c[0], g_mlp[0], w1[0], w2[0], g_final)
```

```python
import functools
import math

import jax
import jax.numpy as jnp
from jax import lax
from jax.experimental import pallas as pl
from jax.experimental.pallas import tpu as pltpu

D_MODEL = 2048
HEAD_DIM = 128
N_ATTN_HEADS = 8
ATTN_WIDTH = N_ATTN_HEADS * HEAD_DIM
CONV_CH = D_MODEL // 2
CONV_K = 31
MOBA_BLOCK = 256
MOBA_TOPK = 3
N_BUCKETS = 32
MAX_DISTANCE = 2048
N_CROSS_HEADS = 4
CROSS_HEAD_DIM = 128
CROSS_WIDTH = N_CROSS_HEADS * CROSS_HEAD_DIM
D_FF = 4 * D_MODEL
EPS = 1e-6

F32 = jnp.float32
BF16 = jnp.bfloat16
NEG_INF = float("-inf")
LOG2E = math.log2(math.e)
Q_PRESCALE = HEAD_DIM ** -0.5 * LOG2E

N_BIAS_TILES = 8
SUBLANES = 8
CONV_CHAINS = 4
CONV_HALO = 32

V7X_VMEM_LIMIT = 56 * 1024 * 1024


def _params(sem):
    return pltpu.CompilerParams(dimension_semantics=sem, vmem_limit_bytes=V7X_VMEM_LIMIT)


def _rms(xf, g):
    return xf * lax.rsqrt(jnp.mean(xf * xf, axis=-1, keepdims=True) + EPS) * g


def _inproj_kernel(x_ref, g_ref, w_ref, qT_ref, k_ref, vT_ref, hg_ref):
    n = _rms(x_ref[...], g_ref[...]).astype(BF16)

    def proj(col):
        return jnp.dot(n, w_ref[:, col * ATTN_WIDTH:(col + 1) * ATTN_WIDTH],
                       preferred_element_type=F32)

    def heads(z):
        return [z[:, h * HEAD_DIM:(h + 1) * HEAD_DIM] for h in range(N_ATTN_HEADS)]

    for h, zh in enumerate(heads(proj(0))):
        qT_ref[h] = (zh * Q_PRESCALE).T.astype(BF16)
    for h, zh in enumerate(heads(proj(1))):
        k_ref[h] = zh.astype(BF16)
    for h, zh in enumerate(heads(proj(2))):
        vT_ref[h] = zh.T.astype(BF16)
    hg_ref[...] = proj(3) * jax.nn.sigmoid(proj(4))


def _inproj(x, g, w_bf16, tm=512):
    B, S, D = x.shape
    head_t = jax.ShapeDtypeStruct((B, N_ATTN_HEADS, HEAD_DIM, S), BF16)
    head_n = jax.ShapeDtypeStruct((B, N_ATTN_HEADS, S, HEAD_DIM), BF16)
    hg = jax.ShapeDtypeStruct((B, S, CONV_CH), F32)
    t_spec = pl.BlockSpec((None, N_ATTN_HEADS, HEAD_DIM, tm), lambda b, i: (b, 0, 0, i))
    n_spec = pl.BlockSpec((None, N_ATTN_HEADS, tm, HEAD_DIM), lambda b, i: (b, 0, i, 0))
    return pl.pallas_call(
        _inproj_kernel,
        grid=(B, S // tm),
        in_specs=[
            pl.BlockSpec((None, tm, D), lambda b, i: (b, i, 0)),
            pl.BlockSpec((1, D), lambda b, i: (0, 0)),
            pl.BlockSpec(w_bf16.shape, lambda b, i: (0, 0), pipeline_mode=pl.Buffered(1)),
        ],
        out_specs=[t_spec, n_spec, t_spec,
                   pl.BlockSpec((None, tm, CONV_CH), lambda b, i: (b, i, 0))],
        out_shape=[head_t, head_n, head_t, hg],
        compiler_params=_params(("arbitrary", "arbitrary")),
        name="inproj",
    )(x, g, w_bf16)


def _bias_kernel(rb_ref, out_ref):
    h = pl.program_id(0)
    koff = lax.broadcasted_iota(jnp.int32, (MOBA_BLOCK, MOBA_BLOCK), 0)
    qoff = lax.broadcasted_iota(jnp.int32, (MOBA_BLOCK, MOBA_BLOCK), 1)
    max_exact = N_BUCKETS // 2
    for delta in range(N_BIAS_TILES):
        dist = jnp.maximum(delta * MOBA_BLOCK + qoff - koff, 0)
        nf = jnp.maximum(dist, max_exact).astype(F32)
        large = max_exact + (jnp.log(nf / max_exact) / math.log(MAX_DISTANCE / max_exact)
                             * (N_BUCKETS - max_exact)).astype(jnp.int32)
        large = jnp.minimum(large, N_BUCKETS - 1)
        bucket = jnp.where(dist < max_exact, dist, large)
        val = jnp.zeros((MOBA_BLOCK, MOBA_BLOCK), F32)
        for b in range(N_BUCKETS):
            val = jnp.where(bucket == b, rb_ref[b, h], val)
        if delta == 0:
            val = jnp.where(koff <= qoff, val, NEG_INF)
        out_ref[delta] = val * LOG2E


def _bias_tiles(rel_bias):
    return pl.pallas_call(
        _bias_kernel,
        grid=(N_ATTN_HEADS,),
        in_specs=[pl.BlockSpec(memory_space=pltpu.SMEM)],
        out_specs=pl.BlockSpec((None, N_BIAS_TILES, MOBA_BLOCK, MOBA_BLOCK),
                               lambda h: (h, 0, 0, 0)),
        out_shape=jax.ShapeDtypeStruct(
            (N_ATTN_HEADS, N_BIAS_TILES, MOBA_BLOCK, MOBA_BLOCK), F32),
        compiler_params=_params(("arbitrary",)),
        name="bias_tiles",
    )(rel_bias)


def _moba_kernel(qT_ref, k_ref, vT_ref, bias_ref, o_ref,
                 kbh_scr, kbl_scr, sel_scr, s_scr, p_scr):
    qi = pl.program_id(2)
    nb = kbh_scr.shape[0]

    @pl.when(qi == 0)
    def _():
        def mean_body(n, carry):
            blk = k_ref[pl.ds(pl.multiple_of(n * MOBA_BLOCK, MOBA_BLOCK), MOBA_BLOCK), :]
            kbar = jnp.sum(blk.astype(F32), axis=0, keepdims=True) * (1.0 / MOBA_BLOCK)
            hi = kbar.astype(BF16)
            kbh_scr[pl.ds(n, 1), :] = hi.astype(F32)
            kbl_scr[pl.ds(n, 1), :] = kbar - hi.astype(F32)
            return carry
        lax.fori_loop(0, nb, mean_body, 0)

    qT = qT_ref[...]

    gate = (jnp.dot(kbh_scr[...].astype(BF16), qT, preferred_element_type=F32)
            + jnp.dot(kbl_scr[...].astype(BF16), qT, preferred_element_type=F32))
    blk_id = lax.broadcasted_iota(jnp.int32, gate.shape, 0)
    gate = jnp.where(blk_id < qi, gate, NEG_INF)
    selected = blk_id == qi
    for _ in range(MOBA_TOPK):
        mx = jnp.max(gate, axis=0, keepdims=True)
        is_max = jnp.logical_and(gate == mx, mx > NEG_INF)
        first = jnp.min(jnp.where(is_max, blk_id, nb), axis=0, keepdims=True)
        pick = blk_id == first
        selected = jnp.logical_or(selected, pick)
        gate = jnp.where(pick, NEG_INF, gate)
    sel_scr[...] = jnp.where(selected, 0.0, NEG_INF)

    span = 2 * MOBA_BLOCK

    def group_blocks(g):
        top = qi - 2 * g
        lo = jnp.clip(top - 1, 0, nb - 2)
        return top, lo

    def mask_row(top, blk):
        row = sel_scr[pl.ds(blk, 1), :]
        return jnp.where(blk <= top, row, NEG_INF)

    def score(g):
        _, lo = group_blocks(g)
        r0 = pl.multiple_of(lo * MOBA_BLOCK, MOBA_BLOCK)
        return jnp.dot(k_ref[pl.ds(r0, span), :], qT, preferred_element_type=F32)

    def prep(s_all, g, slot):
        top, lo = group_blocks(g)
        mx = None
        for t in range(2):
            blk = lo + t
            rows = slice(t * MOBA_BLOCK, (t + 1) * MOBA_BLOCK)
            x = s_all[rows] + bias_ref[jnp.clip(qi - blk, 0, N_BIAS_TILES - 1)]
            s_scr[slot, rows, :] = x
            bmx = jnp.max(x, axis=0, keepdims=True) + mask_row(top, blk)
            mx = bmx if mx is None else jnp.maximum(mx, bmx)
        return mx

    def weighted_values(g, slot):
        _, lo = group_blocks(g)
        r0 = pl.multiple_of(lo * MOBA_BLOCK, MOBA_BLOCK)
        return jnp.dot(vT_ref[:, pl.ds(r0, span)], p_scr[slot], preferred_element_type=F32)

    def softmax(g, slot, m, l, mx):
        top, lo = group_blocks(g)
        m_new = jnp.maximum(m, mx)
        alpha = jnp.exp2(m - m_new)
        psum = None
        for t in range(2):
            rows = slice(t * MOBA_BLOCK, (t + 1) * MOBA_BLOCK)
            p = jnp.exp2(s_scr[slot, rows, :] - (m_new - mask_row(top, lo + t)))
            p_scr[slot, rows, :] = p.astype(BF16)
            psum = p if psum is None else psum + p
        return m_new, alpha * l + jnp.sum(psum, axis=0, keepdims=True), alpha

    def body(t, carry):
        m, l, acc, mx0, mx1, a0, a1 = carry
        g = 2 * t
        s0, s1 = score(g + 2), score(g + 3)
        pv0, pv1 = weighted_values(g - 2, 0), weighted_values(g - 1, 1)
        acc = a1 * (a0 * acc + pv0) + pv1
        m, l, a0 = softmax(g, 0, m, l, mx0)
        m, l, a1 = softmax(g + 1, 1, m, l, mx1)
        return m, l, acc, prep(s0, g + 2, 0), prep(s1, g + 3, 1), a0, a1

    p_scr[...] = jnp.zeros(p_scr.shape, BF16)
    n_groups = qi // 2 + 1
    n_trips = (n_groups + 1) // 2
    row = lambda v: jnp.full((1, MOBA_BLOCK), v, F32)
    init = (row(NEG_INF), row(0.0), jnp.zeros((HEAD_DIM, MOBA_BLOCK), F32),
            prep(score(0), 0, 0), prep(score(1), 1, 1), row(1.0), row(1.0))
    m, l, acc, _, _, a0, a1 = lax.fori_loop(0, n_trips, body, init)
    g = 2 * n_trips
    acc = a1 * (a0 * acc + weighted_values(g - 2, 0)) + weighted_values(g - 1, 1)
    o_ref[...] = (acc / l).T.astype(o_ref.dtype)


def _moba(qT, k, vT, bias):
    B, H, _, S = qT.shape
    nb = S // MOBA_BLOCK
    return pl.pallas_call(
        _moba_kernel,
        grid=(B, H, nb),
        in_specs=[
            pl.BlockSpec((None, None, HEAD_DIM, MOBA_BLOCK), lambda b, h, i: (b, h, 0, i)),
            pl.BlockSpec((None, None, S, HEAD_DIM), lambda b, h, i: (b, h, 0, 0)),
            pl.BlockSpec((None, None, HEAD_DIM, S), lambda b, h, i: (b, h, 0, 0)),
            pl.BlockSpec((None, N_BIAS_TILES, MOBA_BLOCK, MOBA_BLOCK),
                         lambda b, h, i: (h, 0, 0, 0)),
        ],
        out_specs=pl.BlockSpec((None, MOBA_BLOCK, HEAD_DIM), lambda b, h, i: (b, i, h)),
        out_shape=jax.ShapeDtypeStruct((B, S, ATTN_WIDTH), BF16),
        scratch_shapes=[pltpu.VMEM((nb, HEAD_DIM), F32), pltpu.VMEM((nb, HEAD_DIM), F32),
                        pltpu.VMEM((nb, MOBA_BLOCK), F32),
                        pltpu.VMEM((2, 2 * MOBA_BLOCK, MOBA_BLOCK), F32),
                        pltpu.VMEM((2, 2 * MOBA_BLOCK, MOBA_BLOCK), BF16)],
        compiler_params=_params(("arbitrary", "arbitrary", "arbitrary")),
        name="moba",
    )(qT, k, vT, bias)


def _conv_kernel(cur_ref, halo_ref, w_ref, b_ref, g_ref, beta_ref, o_ref, xs_scr, y_scr):
    i = pl.program_id(1)
    ts = cur_ref.shape[0]
    rc = 32
    n_rows = ts + CONV_HALO
    n_shift_rows = n_rows - SUBLANES
    first_tap = CONV_HALO - (CONV_K - 1)
    for c in range(CONV_CH // 128):
        lanes = slice(c * 128, (c + 1) * 128)
        xs_scr[c, 0, 0:CONV_HALO, :] = jnp.where(i == 0, 0.0, halo_ref[:, lanes])
        xs_scr[c, 0, CONV_HALO:n_rows, :] = cur_ref[:, lanes]
        for s in range(1, SUBLANES):
            xs_scr[c, s, 0:n_shift_rows, :] = xs_scr[c, 0, s:s + n_shift_rows, :]
        w_tiles = [jnp.broadcast_to(w_ref[t:t + 1, lanes], (SUBLANES, 128))
                   for t in range(CONV_K)]
        b_tile = jnp.broadcast_to(b_ref[:, lanes], (SUBLANES, 128))

        def rows_body(r, carry):
            for u in range(rc // SUBLANES):
                row0 = pl.multiple_of(r * rc + u * SUBLANES, SUBLANES)
                parts = [b_tile] + [None] * (CONV_CHAINS - 1)
                for t in range(CONV_K):
                    s, a = (first_tap + t) % SUBLANES, (first_tap + t) // SUBLANES
                    term = w_tiles[t] * xs_scr[c, s, pl.ds(row0 + SUBLANES * a, SUBLANES), :]
                    i_part = t % CONV_CHAINS
                    parts[i_part] = term if parts[i_part] is None else parts[i_part] + term
                while len(parts) > 1:
                    parts = [parts[i] + parts[i + 1] for i in range(0, len(parts), 2)]
                y_scr[pl.ds(row0, SUBLANES), lanes] = parts[0]
            return carry

        lax.fori_loop(0, ts // rc, rows_body, 0)
    y = y_scr[...]
    mu = jnp.mean(y, axis=-1, keepdims=True)
    yc = y - mu
    var = jnp.mean(yc * yc, axis=-1, keepdims=True)
    z = yc * lax.rsqrt(var + EPS) * g_ref[...] + beta_ref[...]
    o_ref[...] = (z * jax.nn.sigmoid(z)).astype(o_ref.dtype)


def _odd_tiles(rows):
    tiles = -(-rows // SUBLANES)
    return (tiles + 1 - tiles % 2) * SUBLANES


def _conv_branch(hg, conv_w, conv_b, ln_g, ln_b, ts=256):
    B, S, C = hg.shape
    halo_blocks = ts // CONV_HALO
    vec = pl.BlockSpec((1, C), lambda b, i: (0, 0))
    return pl.pallas_call(
        _conv_kernel,
        grid=(B, S // ts),
        in_specs=[
            pl.BlockSpec((None, ts, C), lambda b, i: (b, i, 0)),
            pl.BlockSpec((None, CONV_HALO, C),
                         lambda b, i: (b, jnp.maximum(i * halo_blocks - 1, 0), 0)),
            pl.BlockSpec((CONV_K, C), lambda b, i: (0, 0)),
            vec, vec, vec,
        ],
        out_specs=pl.BlockSpec((None, ts, C), lambda b, i: (b, i, 0)),
        out_shape=jax.ShapeDtypeStruct((B, S, C), BF16),
        scratch_shapes=[pltpu.VMEM((C // 128, SUBLANES, _odd_tiles(ts + CONV_HALO), 128), F32),
                        pltpu.VMEM((ts, C), F32)],
        compiler_params=_params(("arbitrary", "arbitrary")),
        name="conv_branch",
    )(hg, hg, conv_w, conv_b, ln_g, ln_b)


def _outproj_kernel(x_ref, a_ref, c_ref, wa_ref, wc_ref, o_ref):
    o_ref[...] = (x_ref[...]
                  + jnp.dot(a_ref[...], wa_ref[...], preferred_element_type=F32)
                  + jnp.dot(c_ref[...], wc_ref[...], preferred_element_type=F32))


def _outproj(x2, a2, c2, w_out_bf16, tm=512, tn=1024):
    M, D = x2.shape
    return pl.pallas_call(
        _outproj_kernel,
        grid=(M // tm, D // tn),
        in_specs=[
            pl.BlockSpec((tm, tn), lambda i, j: (i, j)),
            pl.BlockSpec((tm, ATTN_WIDTH), lambda i, j: (i, 0)),
            pl.BlockSpec((tm, CONV_CH), lambda i, j: (i, 0)),
            pl.BlockSpec((ATTN_WIDTH, tn), lambda i, j: (0, j)),
            pl.BlockSpec((CONV_CH, tn), lambda i, j: (1, j)),
        ],
        out_specs=pl.BlockSpec((tm, tn), lambda i, j: (i, j)),
        out_shape=jax.ShapeDtypeStruct((M, D), F32),
        compiler_params=_params(("arbitrary", "arbitrary")),
        name="outproj",
    )(x2, a2, c2, w_out_bf16, w_out_bf16)


def _memkv_kernel(m_ref, g_ref, wk_ref, wv_ref, k_ref, v_ref):
    n = _rms(m_ref[...], g_ref[...]).astype(BF16)
    k_ref[...] = jnp.dot(n, wk_ref[...], preferred_element_type=F32).astype(BF16)
    v_ref[...] = jnp.dot(n, wv_ref[...], preferred_element_type=F32).astype(BF16)


def _memkv(mem2, g, wk, wv):
    M, D = mem2.shape
    full = lambda shape: pl.BlockSpec(shape, lambda i: (0, 0))
    out = jax.ShapeDtypeStruct((M, CROSS_WIDTH), BF16)
    return pl.pallas_call(
        _memkv_kernel,
        grid=(1,),
        in_specs=[full((M, D)), full((1, D)), full((D, CROSS_WIDTH)), full((D, CROSS_WIDTH))],
        out_specs=[full((M, CROSS_WIDTH)), full((M, CROSS_WIDTH))],
        out_shape=[out, out],
        compiler_params=_params(("arbitrary",)),
        name="mem_kv",
    )(mem2, g, wk, wv)


def _cross_kernel(h_ref, g_ref, wq_ref, kc_ref, vc_ref, wo_ref, o_ref):
    h = h_ref[...]
    n = _rms(h, g_ref[...]).astype(BF16)
    q = jnp.dot(n, wq_ref[...], preferred_element_type=F32).astype(BF16)
    scale = CROSS_HEAD_DIM ** -0.5
    outs = []
    for hd in range(N_CROSS_HEADS):
        cols = slice(hd * CROSS_HEAD_DIM, (hd + 1) * CROSS_HEAD_DIM)
        s = lax.dot_general(q[:, cols], kc_ref[:, cols], (((1,), (1,)), ((), ())),
                            preferred_element_type=F32) * scale
        m = jnp.max(s, axis=-1, keepdims=True)
        p = jnp.exp(s - m)
        p = p / jnp.sum(p, axis=-1, keepdims=True)
        outs.append(jnp.dot(p.astype(BF16), vc_ref[:, cols], preferred_element_type=F32))
    o = jnp.concatenate(outs, axis=-1).astype(BF16)
    o_ref[...] = h + jnp.dot(o, wo_ref[...], preferred_element_type=F32)


def _cross(h1, g, wq, kc, vc, wo, tm=512):
    B, S, D = h1.shape
    n_mem = kc.shape[1]
    const = lambda shape: pl.BlockSpec(shape, lambda b, i: (0, 0))
    return pl.pallas_call(
        _cross_kernel,
        grid=(B, S // tm),
        in_specs=[
            pl.BlockSpec((None, tm, D), lambda b, i: (b, i, 0)),
            const((1, D)),
            const((D, CROSS_WIDTH)),
            pl.BlockSpec((None, n_mem, CROSS_WIDTH), lambda b, i: (b, 0, 0)),
            pl.BlockSpec((None, n_mem, CROSS_WIDTH), lambda b, i: (b, 0, 0)),
            const((CROSS_WIDTH, D)),
        ],
        out_specs=pl.BlockSpec((None, tm, D), lambda b, i: (b, i, 0)),
        out_shape=jax.ShapeDtypeStruct((B, S, D), F32),
        compiler_params=_params(("arbitrary", "arbitrary")),
        name="cross_attn",
    )(h1, g, wq, kc, vc, wo)


def _mlp_kernel(h_ref, g_ref, w1_ref, w2_ref, gf_ref, o_ref, n_scr, acc_scr):
    f = pl.program_id(1)

    @pl.when(f == 0)
    def _():
        h = h_ref[...]
        n_scr[...] = _rms(h, g_ref[...]).astype(BF16)
        acc_scr[...] = h

    u = jnp.dot(n_scr[...], w1_ref[...], preferred_element_type=F32)
    u = jnp.square(jnp.maximum(u, 0.0)).astype(BF16)
    acc_scr[...] += jnp.dot(u, w2_ref[...], preferred_element_type=F32)

    @pl.when(f == pl.num_programs(1) - 1)
    def _():
        o_ref[...] = _rms(acc_scr[...], gf_ref[...])


def _mlp(h2, g, w1, w2, g_final, tm=512, tf=1024):
    M, D = h2.shape
    vec = pl.BlockSpec((1, D), lambda i, f: (0, 0))
    return pl.pallas_call(
        _mlp_kernel,
        grid=(M // tm, D_FF // tf),
        in_specs=[
            pl.BlockSpec((tm, D), lambda i, f: (i, 0)),
            vec,
            pl.BlockSpec((D, tf), lambda i, f: (0, f)),
            pl.BlockSpec((tf, D), lambda i, f: (f, 0)),
            vec,
        ],
        out_specs=pl.BlockSpec((tm, D), lambda i, f: (i, 0)),
        out_shape=jax.ShapeDtypeStruct((M, D), F32),
        scratch_shapes=[pltpu.VMEM((tm, D), BF16), pltpu.VMEM((tm, D), F32)],
        compiler_params=_params(("arbitrary", "arbitrary")),
        name="mlp",
    )(h2, g, w1, w2, g_final)


@jax.jit
def _layer(x, mem, g_mix, w_in, conv_w, conv_b, conv_ln_g, conv_ln_b, w_out, rel_bias,
           g_cross, g_mem, wq_c, wk_c, wv_c, wo_c, g_mlp, w1, w2, g_final):
    B, S, D = x.shape
    row = lambda v: v.reshape(1, -1)
    bias = _bias_tiles(rel_bias)
    qT, k, vT, hg = _inproj(x, row(g_mix), w_in.astype(BF16))
    a = _moba(qT, k, vT, bias)
    c = _conv_branch(hg, conv_w, row(conv_b), row(conv_ln_g), row(conv_ln_b))
    h1 = _outproj(x.reshape(B * S, D), a.reshape(B * S, ATTN_WIDTH),
                  c.reshape(B * S, CONV_CH), w_out.astype(BF16))
    kc, vc = _memkv(mem.reshape(-1, D), row(g_mem), wk_c.astype(BF16), wv_c.astype(BF16))
    n_mem = mem.shape[1]
    h2 = _cross(h1.reshape(B, S, D), row(g_cross), wq_c.astype(BF16),
                kc.reshape(B, n_mem, CROSS_WIDTH), vc.reshape(B, n_mem, CROSS_WIDTH),
                wo_c.astype(BF16))
    out = _mlp(h2.reshape(B * S, D), row(g_mlp), w1.astype(BF16), w2.astype(BF16),
               row(g_final))
    return out.reshape(B, S, D)


def kernel(x, mem, g_mix, w_in, conv_w, conv_b, conv_ln_g, conv_ln_b, w_out, rel_bias,
           g_cross, g_mem, wq_c, wk_c, wv_c, wo_c, g_mlp, w1, w2, g_final):
    return _layer(x, mem, g_mix[0], w_in[0], conv_w[0], conv_b[0], conv_ln_g[0], conv_ln_b[0],
                  w_out[0], rel_bias, g_cross[0], g_mem[0], wq_c[0], wk_c[0], wv_c[0],
                  wo_c[0], g_mlp[0], w1[0], w2[0], g_final)
```

```python
import functools
import math

import jax
import jax.numpy as jnp
from jax import lax
from jax.experimental import pallas as pl
from jax.experimental.pallas import tpu as pltpu

D_MODEL = 2048
HEAD_DIM = 128
N_ATTN_HEADS = 8
ATTN_WIDTH = N_ATTN_HEADS * HEAD_DIM
CONV_CH = D_MODEL // 2
CONV_K = 31
MOBA_BLOCK = 256
MOBA_TOPK = 3
N_BUCKETS = 32
MAX_DISTANCE = 2048
N_CROSS_HEADS = 4
CROSS_HEAD_DIM = 128
CROSS_WIDTH = N_CROSS_HEADS * CROSS_HEAD_DIM
D_FF = 4 * D_MODEL
EPS = 1e-6

F32 = jnp.float32
BF16 = jnp.bfloat16
NEG_INF = float("-inf")
LOG2E = math.log2(math.e)
Q_PRESCALE = HEAD_DIM ** -0.5 * LOG2E

N_BIAS_TILES = 8
SUBLANES = 8
TRIP_GROUPS = 2
ROW_HALVES = 2
CONV_CHAINS = 4
CONV_HALO = 32

V7X_VMEM_LIMIT = 56 * 1024 * 1024


def _params(sem):
    return pltpu.CompilerParams(dimension_semantics=sem, vmem_limit_bytes=V7X_VMEM_LIMIT)


def _rms(xf, g):
    return xf * lax.rsqrt(jnp.mean(xf * xf, axis=-1, keepdims=True) + EPS) * g


def _inproj_kernel(x_ref, g_ref, w_ref, qT_ref, k_ref, vT_ref, hg_ref):
    n = _rms(x_ref[...], g_ref[...]).astype(BF16)

    def proj(col):
        return jnp.dot(n, w_ref[:, col * ATTN_WIDTH:(col + 1) * ATTN_WIDTH],
                       preferred_element_type=F32)

    def heads(z):
        return [z[:, h * HEAD_DIM:(h + 1) * HEAD_DIM] for h in range(N_ATTN_HEADS)]

    for h, zh in enumerate(heads(proj(0))):
        qT_ref[h] = (zh * Q_PRESCALE).T.astype(BF16)
    for h, zh in enumerate(heads(proj(1))):
        k_ref[h] = zh.astype(BF16)
    for h, zh in enumerate(heads(proj(2))):
        vT_ref[h] = zh.T.astype(BF16)
    hg_ref[...] = proj(3) * jax.nn.sigmoid(proj(4))


def _inproj(x, g, w_bf16, tm=512):
    B, S, D = x.shape
    head_t = jax.ShapeDtypeStruct((B, N_ATTN_HEADS, HEAD_DIM, S), BF16)
    head_n = jax.ShapeDtypeStruct((B, N_ATTN_HEADS, S, HEAD_DIM), BF16)
    hg = jax.ShapeDtypeStruct((B, S, CONV_CH), F32)
    t_spec = pl.BlockSpec((None, N_ATTN_HEADS, HEAD_DIM, tm), lambda b, i: (b, 0, 0, i))
    n_spec = pl.BlockSpec((None, N_ATTN_HEADS, tm, HEAD_DIM), lambda b, i: (b, 0, i, 0))
    return pl.pallas_call(
        _inproj_kernel,
        grid=(B, S // tm),
        in_specs=[
            pl.BlockSpec((None, tm, D), lambda b, i: (b, i, 0)),
            pl.BlockSpec((1, D), lambda b, i: (0, 0)),
            pl.BlockSpec(w_bf16.shape, lambda b, i: (0, 0), pipeline_mode=pl.Buffered(1)),
        ],
        out_specs=[t_spec, n_spec, t_spec,
                   pl.BlockSpec((None, tm, CONV_CH), lambda b, i: (b, i, 0))],
        out_shape=[head_t, head_n, head_t, hg],
        compiler_params=_params(("arbitrary", "arbitrary")),
        name="inproj",
    )(x, g, w_bf16)


def _bias_kernel(rb_ref, out_ref):
    h = pl.program_id(0)
    koff = lax.broadcasted_iota(jnp.int32, (MOBA_BLOCK, MOBA_BLOCK), 0)
    qoff = lax.broadcasted_iota(jnp.int32, (MOBA_BLOCK, MOBA_BLOCK), 1)
    max_exact = N_BUCKETS // 2
    for delta in range(N_BIAS_TILES):
        dist = jnp.maximum(delta * MOBA_BLOCK + qoff - koff, 0)
        nf = jnp.maximum(dist, max_exact).astype(F32)
        large = max_exact + (jnp.log(nf / max_exact) / math.log(MAX_DISTANCE / max_exact)
                             * (N_BUCKETS - max_exact)).astype(jnp.int32)
        large = jnp.minimum(large, N_BUCKETS - 1)
        bucket = jnp.where(dist < max_exact, dist, large)
        val = jnp.zeros((MOBA_BLOCK, MOBA_BLOCK), F32)
        for b in range(N_BUCKETS):
            val = jnp.where(bucket == b, rb_ref[b, h], val)
        if delta == 0:
            val = jnp.where(koff <= qoff, val, NEG_INF)
        out_ref[delta] = (val * LOG2E).astype(out_ref.dtype)


def _bias_tiles(rel_bias):
    return pl.pallas_call(
        _bias_kernel,
        grid=(N_ATTN_HEADS,),
        in_specs=[pl.BlockSpec(memory_space=pltpu.SMEM)],
        out_specs=pl.BlockSpec((None, N_BIAS_TILES, MOBA_BLOCK, MOBA_BLOCK),
                               lambda h: (h, 0, 0, 0)),
        out_shape=jax.ShapeDtypeStruct(
            (N_ATTN_HEADS, N_BIAS_TILES, MOBA_BLOCK, MOBA_BLOCK), F32),
        compiler_params=_params(("arbitrary",)),
        name="bias_tiles",
    )(rel_bias)


def _moba_kernel(qT_ref, k_ref, vT_ref, bias_ref, o_ref,
                 kbh_scr, kbl_scr, sel_scr, s_scr, p_scr):
    qi = pl.program_id(2)
    nb = kbh_scr.shape[0]

    @pl.when(qi == 0)
    def _():
        def mean_body(n, carry):
            blk = k_ref[pl.ds(pl.multiple_of(n * MOBA_BLOCK, MOBA_BLOCK), MOBA_BLOCK), :]
            kbar = jnp.sum(blk.astype(F32), axis=0, keepdims=True) * (1.0 / MOBA_BLOCK)
            hi = kbar.astype(BF16)
            kbh_scr[pl.ds(n, 1), :] = hi.astype(F32)
            kbl_scr[pl.ds(n, 1), :] = kbar - hi.astype(F32)
            return carry
        lax.fori_loop(0, nb, mean_body, 0)

    qT = qT_ref[...]

    gate = (jnp.dot(kbh_scr[...].astype(BF16), qT, preferred_element_type=F32)
            + jnp.dot(kbl_scr[...].astype(BF16), qT, preferred_element_type=F32))
    blk_id = lax.broadcasted_iota(jnp.int32, gate.shape, 0)
    gate = jnp.where(blk_id < qi, gate, NEG_INF)
    selected = blk_id == qi
    for _ in range(MOBA_TOPK):
        mx = jnp.max(gate, axis=0, keepdims=True)
        is_max = jnp.logical_and(gate == mx, mx > NEG_INF)
        first = jnp.min(jnp.where(is_max, blk_id, nb), axis=0, keepdims=True)
        pick = blk_id == first
        selected = jnp.logical_or(selected, pick)
        gate = jnp.where(pick, NEG_INF, gate)
    sel_scr[...] = jnp.where(selected, 0.0, NEG_INF)

    span = 2 * MOBA_BLOCK

    def group_blocks(g):
        top = qi - 2 * g
        lo = jnp.clip(top - 1, 0, nb - 2)
        return top, lo

    def mask_row(top, blk):
        row = sel_scr[pl.ds(blk, 1), :]
        return jnp.where(blk <= top, row, NEG_INF)

    def score(g):
        _, lo = group_blocks(g)
        r0 = pl.multiple_of(lo * MOBA_BLOCK, MOBA_BLOCK)
        return jnp.dot(k_ref[pl.ds(r0, span), :], qT, preferred_element_type=F32)

    def prep(s_all, g, slot):
        top, lo = group_blocks(g)
        mx = None
        for t in range(2):
            blk = lo + t
            rows = slice(t * MOBA_BLOCK, (t + 1) * MOBA_BLOCK)
            x = s_all[rows] + bias_ref[jnp.clip(qi - blk, 0, N_BIAS_TILES - 1)]
            s_scr[slot, rows, :] = x
            bmx = jnp.max(x, axis=0, keepdims=True) + mask_row(top, blk)
            mx = bmx if mx is None else jnp.maximum(mx, bmx)
        return mx

    def weighted_values(g, slot):
        _, lo = group_blocks(g)
        r0 = pl.multiple_of(lo * MOBA_BLOCK, MOBA_BLOCK)
        return jnp.dot(vT_ref[:, pl.ds(r0, span)], p_scr[slot], preferred_element_type=F32)

    def softmax(g, slot, m, l, mx):
        top, lo = group_blocks(g)
        m_new = jnp.maximum(m, mx)
        alpha = jnp.exp2(m - m_new)
        psum = None
        for t in range(2):
            rows = slice(t * MOBA_BLOCK, (t + 1) * MOBA_BLOCK)
            shift = m_new - mask_row(top, lo + t)
            p = jnp.exp2(s_scr[slot, rows, :] - shift)
            p_scr[slot, rows, :] = p.astype(BF16)
            psum = p if psum is None else psum + p
        return m_new, alpha * l + jnp.sum(psum, axis=0, keepdims=True), alpha

    slots = range(TRIP_GROUPS)

    def accumulate(acc, g, alphas):
        pvs = [weighted_values(g + i, i) for i in slots]
        for a, pv in zip(alphas, pvs):
            acc = a * acc + pv
        return acc

    def body(t, carry):
        m, l, acc, mxs, alphas = carry
        g = TRIP_GROUPS * t
        scores = [score(g + TRIP_GROUPS + i) for i in slots]
        acc = accumulate(acc, g - TRIP_GROUPS, alphas)
        new_alphas = []
        for i in slots:
            m, l, a = softmax(g + i, i, m, l, mxs[i])
            new_alphas.append(a)
        new_mxs = [prep(scores[i], g + TRIP_GROUPS + i, i) for i in slots]
        return m, l, acc, tuple(new_mxs), tuple(new_alphas)

    p_scr[...] = jnp.zeros(p_scr.shape, BF16)
    n_groups = qi // 2 + 1
    n_trips = (n_groups + TRIP_GROUPS - 1) // TRIP_GROUPS
    row = lambda v: jnp.full((1, MOBA_BLOCK), v, F32)
    init = (row(NEG_INF), row(0.0), jnp.zeros((HEAD_DIM, MOBA_BLOCK), F32),
            tuple(prep(score(i), i, i) for i in slots), tuple(row(1.0) for _ in slots))
    m, l, acc, _, alphas = lax.fori_loop(0, n_trips, body, init)
    acc = accumulate(acc, TRIP_GROUPS * (n_trips - 1), alphas)
    o_ref[...] = (acc / l).T.astype(o_ref.dtype)


def _moba(qT, k, vT, bias):
    B, H, _, S = qT.shape
    nb = S // MOBA_BLOCK
    return pl.pallas_call(
        _moba_kernel,
        grid=(B, H, nb),
        in_specs=[
            pl.BlockSpec((None, None, HEAD_DIM, MOBA_BLOCK), lambda b, h, i: (b, h, 0, i)),
            pl.BlockSpec((None, None, S, HEAD_DIM), lambda b, h, i: (b, h, 0, 0)),
            pl.BlockSpec((None, None, HEAD_DIM, S), lambda b, h, i: (b, h, 0, 0)),
            pl.BlockSpec((None, N_BIAS_TILES, MOBA_BLOCK, MOBA_BLOCK),
                         lambda b, h, i: (h, 0, 0, 0)),
        ],
        out_specs=pl.BlockSpec((None, MOBA_BLOCK, HEAD_DIM), lambda b, h, i: (b, i, h)),
        out_shape=jax.ShapeDtypeStruct((B, S, ATTN_WIDTH), BF16),
        scratch_shapes=[pltpu.VMEM((nb, HEAD_DIM), F32), pltpu.VMEM((nb, HEAD_DIM), F32),
                        pltpu.VMEM((nb, MOBA_BLOCK), F32),
                        pltpu.VMEM((TRIP_GROUPS, 2 * MOBA_BLOCK, MOBA_BLOCK), F32),
                        pltpu.VMEM((TRIP_GROUPS, 2 * MOBA_BLOCK, MOBA_BLOCK), BF16)],
        compiler_params=_params(("arbitrary", "arbitrary", "arbitrary")),
        name="moba",
    )(qT, k, vT, bias)


def _conv_kernel(cur_ref, halo_ref, w_ref, b_ref, g_ref, beta_ref, o_ref, xs_scr, y_scr):
    i = pl.program_id(1)
    ts = cur_ref.shape[0]
    rc = 32
    n_rows = ts + CONV_HALO
    n_shift_rows = n_rows - SUBLANES
    first_tap = CONV_HALO - (CONV_K - 1)
    for c in range(CONV_CH // 128):
        lanes = slice(c * 128, (c + 1) * 128)
        xs_scr[c, 0, 0:CONV_HALO, :] = jnp.where(i == 0, 0.0, halo_ref[:, lanes])
        xs_scr[c, 0, CONV_HALO:n_rows, :] = cur_ref[:, lanes]
        for s in range(1, SUBLANES):
            xs_scr[c, s, 0:n_shift_rows, :] = xs_scr[c, 0, s:s + n_shift_rows, :]
        w_tiles = [jnp.broadcast_to(w_ref[t:t + 1, lanes], (SUBLANES, 128))
                   for t in range(CONV_K)]
        b_tile = jnp.broadcast_to(b_ref[:, lanes], (SUBLANES, 128))

        def rows_body(r, carry):
            for u in range(rc // SUBLANES):
                row0 = pl.multiple_of(r * rc + u * SUBLANES, SUBLANES)
                parts = [b_tile] + [None] * (CONV_CHAINS - 1)
                for t in range(CONV_K):
                    s, a = (first_tap + t) % SUBLANES, (first_tap + t) // SUBLANES
                    term = w_tiles[t] * xs_scr[c, s, pl.ds(row0 + SUBLANES * a, SUBLANES), :]
                    i_part = t % CONV_CHAINS
                    parts[i_part] = term if parts[i_part] is None else parts[i_part] + term
                while len(parts) > 1:
                    parts = [parts[i] + parts[i + 1] for i in range(0, len(parts), 2)]
                y_scr[pl.ds(row0, SUBLANES), lanes] = parts[0]
            return carry

        lax.fori_loop(0, ts // rc, rows_body, 0)
    y = y_scr[...]
    mu = jnp.mean(y, axis=-1, keepdims=True)
    yc = y - mu
    var = jnp.mean(yc * yc, axis=-1, keepdims=True)
    z = yc * lax.rsqrt(var + EPS) * g_ref[...] + beta_ref[...]
    o_ref[...] = (z * jax.nn.sigmoid(z)).astype(o_ref.dtype)


def _odd_tiles(rows):
    tiles = -(-rows // SUBLANES)
    return (tiles + 1 - tiles % 2) * SUBLANES


def _conv_branch(hg, conv_w, conv_b, ln_g, ln_b, ts=256):
    B, S, C = hg.shape
    halo_blocks = ts // CONV_HALO
    vec = pl.BlockSpec((1, C), lambda b, i: (0, 0))
    return pl.pallas_call(
        _conv_kernel,
        grid=(B, S // ts),
        in_specs=[
            pl.BlockSpec((None, ts, C), lambda b, i: (b, i, 0)),
            pl.BlockSpec((None, CONV_HALO, C),
                         lambda b, i: (b, jnp.maximum(i * halo_blocks - 1, 0), 0)),
            pl.BlockSpec((CONV_K, C), lambda b, i: (0, 0)),
            vec, vec, vec,
        ],
        out_specs=pl.BlockSpec((None, ts, C), lambda b, i: (b, i, 0)),
        out_shape=jax.ShapeDtypeStruct((B, S, C), BF16),
        scratch_shapes=[pltpu.VMEM((C // 128, SUBLANES, _odd_tiles(ts + CONV_HALO), 128), F32),
                        pltpu.VMEM((ts, C), F32)],
        compiler_params=_params(("arbitrary", "arbitrary")),
        name="conv_branch",
    )(hg, hg, conv_w, conv_b, ln_g, ln_b)


def _memkv_kernel(m_ref, g_ref, wk_ref, wv_ref, k_ref, v_ref):
    n = _rms(m_ref[...], g_ref[...]).astype(BF16)
    k_ref[...] = jnp.dot(n, wk_ref[...], preferred_element_type=F32).astype(BF16)
    v_ref[...] = jnp.dot(n, wv_ref[...], preferred_element_type=F32).astype(BF16)


def _memkv(mem2, g, wk, wv):
    M, D = mem2.shape
    full = lambda shape: pl.BlockSpec(shape, lambda i: (0, 0))
    out = jax.ShapeDtypeStruct((M, CROSS_WIDTH), BF16)
    return pl.pallas_call(
        _memkv_kernel,
        grid=(1,),
        in_specs=[full((M, D)), full((1, D)), full((D, CROSS_WIDTH)), full((D, CROSS_WIDTH))],
        out_specs=[full((M, CROSS_WIDTH)), full((M, CROSS_WIDTH))],
        out_shape=[out, out],
        compiler_params=_params(("arbitrary",)),
        name="mem_kv",
    )(mem2, g, wk, wv)


def _mix_cross_kernel(x_ref, a_ref, c_ref, wout_ref, g_ref, wq_ref, kc_ref, vc_ref, wo_ref,
                      o_ref):
    scale = CROSS_HEAD_DIM ** -0.5
    half = x_ref.shape[0] // ROW_HALVES
    for r in range(ROW_HALVES):
        rows = slice(r * half, (r + 1) * half)
        h1 = (x_ref[rows, :]
              + jnp.dot(a_ref[rows, :], wout_ref[0:ATTN_WIDTH, :], preferred_element_type=F32)
              + jnp.dot(c_ref[rows, :], wout_ref[ATTN_WIDTH:, :], preferred_element_type=F32))
        n = _rms(h1, g_ref[...]).astype(BF16)
        q = jnp.dot(n, wq_ref[...], preferred_element_type=F32).astype(BF16)
        outs = []
        for hd in range(N_CROSS_HEADS):
            cols = slice(hd * CROSS_HEAD_DIM, (hd + 1) * CROSS_HEAD_DIM)
            s = lax.dot_general(q[:, cols], kc_ref[:, cols], (((1,), (1,)), ((), ())),
                                preferred_element_type=F32) * scale
            m = jnp.max(s, axis=-1, keepdims=True)
            p = jnp.exp(s - m)
            p = p / jnp.sum(p, axis=-1, keepdims=True)
            outs.append(jnp.dot(p.astype(BF16), vc_ref[:, cols], preferred_element_type=F32))
        o = jnp.concatenate(outs, axis=-1).astype(BF16)
        o_ref[rows, :] = h1 + jnp.dot(o, wo_ref[...], preferred_element_type=F32)


def _mix_cross(x, a, c, w_out, g, wq, kc, vc, wo, tm=512):
    B, S, D = x.shape
    n_mem = kc.shape[1]
    resident = lambda arr: pl.BlockSpec(arr.shape, lambda b, i: (0,) * arr.ndim,
                                        pipeline_mode=pl.Buffered(1))
    rows = lambda width: pl.BlockSpec((None, tm, width), lambda b, i: (b, i, 0))
    per_batch = pl.BlockSpec((None, n_mem, CROSS_WIDTH), lambda b, i: (b, 0, 0))
    return pl.pallas_call(
        _mix_cross_kernel,
        grid=(B, S // tm),
        in_specs=[rows(D), rows(ATTN_WIDTH), rows(CONV_CH), resident(w_out), resident(g),
                  resident(wq), per_batch, per_batch, resident(wo)],
        out_specs=rows(D),
        out_shape=jax.ShapeDtypeStruct((B, S, D), F32),
        compiler_params=_params(("arbitrary", "arbitrary")),
        name="mix_cross",
    )(x, a, c, w_out, g, wq, kc, vc, wo)


def _mlp_kernel(h_ref, g_ref, w1_ref, w2_ref, gf_ref, o_ref, n_scr, acc_scr):
    f = pl.program_id(1)

    @pl.when(f == 0)
    def _():
        h = h_ref[...]
        n_scr[...] = _rms(h, g_ref[...]).astype(BF16)
        acc_scr[...] = h

    u = jnp.dot(n_scr[...], w1_ref[...], preferred_element_type=F32)
    u = jnp.square(jnp.maximum(u, 0.0)).astype(BF16)
    acc_scr[...] += jnp.dot(u, w2_ref[...], preferred_element_type=F32)

    @pl.when(f == pl.num_programs(1) - 1)
    def _():
        o_ref[...] = _rms(acc_scr[...], gf_ref[...])


def _mlp(h2, g, w1, w2, g_final, tm=512, tf=1024):
    M, D = h2.shape
    vec = pl.BlockSpec((1, D), lambda i, f: (0, 0))
    return pl.pallas_call(
        _mlp_kernel,
        grid=(M // tm, D_FF // tf),
        in_specs=[
            pl.BlockSpec((tm, D), lambda i, f: (i, 0)),
            vec,
            pl.BlockSpec((D, tf), lambda i, f: (0, f)),
            pl.BlockSpec((tf, D), lambda i, f: (f, 0)),
            vec,
        ],
        out_specs=pl.BlockSpec((tm, D), lambda i, f: (i, 0)),
        out_shape=jax.ShapeDtypeStruct((M, D), F32),
        scratch_shapes=[pltpu.VMEM((tm, D), BF16), pltpu.VMEM((tm, D), F32)],
        compiler_params=_params(("arbitrary", "arbitrary")),
        name="mlp",
    )(h2, g, w1, w2, g_final)


@jax.jit
def _layer(x, mem, g_mix, w_in, conv_w, conv_b, conv_ln_g, conv_ln_b, w_out, rel_bias,
           g_cross, g_mem, wq_c, wk_c, wv_c, wo_c, g_mlp, w1, w2, g_final):
    B, S, D = x.shape
    row = lambda v: v.reshape(1, -1)
    bias = _bias_tiles(rel_bias)
    qT, k, vT, hg = _inproj(x, row(g_mix), w_in.astype(BF16))
    a = _moba(qT, k, vT, bias)
    c = _conv_branch(hg, conv_w, row(conv_b), row(conv_ln_g), row(conv_ln_b))
    kc, vc = _memkv(mem.reshape(-1, D), row(g_mem), wk_c.astype(BF16), wv_c.astype(BF16))
    n_mem = mem.shape[1]
    h2 = _mix_cross(x, a, c, w_out.astype(BF16), row(g_cross), wq_c.astype(BF16),
                    kc.reshape(B, n_mem, CROSS_WIDTH), vc.reshape(B, n_mem, CROSS_WIDTH),
                    wo_c.astype(BF16))
    out = _mlp(h2.reshape(B * S, D), row(g_mlp), w1.astype(BF16), w2.astype(BF16),
               row(g_final))
    return out.reshape(B, S, D)


def kernel(x, mem, g_mix, w_in, conv_w, conv_b, conv_ln_g, conv_ln_b, w_out, rel_bias,
           g_cross, g_mem, wq_c, wk_c, wv_c, wo_c, g_mlp, w1, w2, g_final):
    return _layer(x, mem, g_mix[0], w_in[0], conv_w[0], conv_b[0], conv_ln_g[0], conv_ln_b[0],
                  w_out[0], rel_bias, g_cross[0], g_mem[0], wq_c[0], wk_c[0], wv_c[0],
                  wo_c[0], g_mlp[0], w1[0], w2[0], g_final)
```

```python
import functools
import math

import jax
import jax.numpy as jnp
from jax import lax
from jax.experimental import pallas as pl
from jax.experimental.pallas import tpu as pltpu

D_MODEL = 2048
HEAD_DIM = 128
N_ATTN_HEADS = 8
ATTN_WIDTH = N_ATTN_HEADS * HEAD_DIM
CONV_CH = D_MODEL // 2
CONV_K = 31
MOBA_BLOCK = 256
MOBA_TOPK = 3
N_BUCKETS = 32
MAX_DISTANCE = 2048
N_CROSS_HEADS = 4
CROSS_HEAD_DIM = 128
CROSS_WIDTH = N_CROSS_HEADS * CROSS_HEAD_DIM
D_FF = 4 * D_MODEL
EPS = 1e-6

F32 = jnp.float32
BF16 = jnp.bfloat16
NEG_INF = float("-inf")
LOG2E = math.log2(math.e)
Q_PRESCALE = HEAD_DIM ** -0.5 * LOG2E

N_BIAS_TILES = 8
SUBLANES = 8
TRIP_GROUPS = 2
ROW_HALVES = 2
CONV_CHAINS = 4
CONV_HALO = 32

V7X_VMEM_LIMIT = 56 * 1024 * 1024


def _params(sem):
    return pltpu.CompilerParams(dimension_semantics=sem, vmem_limit_bytes=V7X_VMEM_LIMIT)


def _rms(xf, g):
    return xf * lax.rsqrt(jnp.mean(xf * xf, axis=-1, keepdims=True) + EPS) * g


def _inproj_kernel(x_ref, g_ref, w_ref, qT_ref, k_ref, vT_ref, hg_ref):
    n = _rms(x_ref[...], g_ref[...]).astype(BF16)

    def proj(col):
        return jnp.dot(n, w_ref[:, col * ATTN_WIDTH:(col + 1) * ATTN_WIDTH],
                       preferred_element_type=F32)

    def heads(z):
        return [z[:, h * HEAD_DIM:(h + 1) * HEAD_DIM] for h in range(N_ATTN_HEADS)]

    for h, zh in enumerate(heads(proj(0))):
        qT_ref[h] = (zh * Q_PRESCALE).T.astype(BF16)
    for h, zh in enumerate(heads(proj(1))):
        k_ref[h] = zh.astype(BF16)
    for h, zh in enumerate(heads(proj(2))):
        vT_ref[h] = zh.T.astype(BF16)
    hg_ref[...] = proj(3) * jax.nn.sigmoid(proj(4))


def _inproj(x, g, w_bf16, tm=512):
    B, S, D = x.shape
    head_t = jax.ShapeDtypeStruct((B, N_ATTN_HEADS, HEAD_DIM, S), BF16)
    head_n = jax.ShapeDtypeStruct((B, N_ATTN_HEADS, S, HEAD_DIM), BF16)
    hg = jax.ShapeDtypeStruct((B, S, CONV_CH), F32)
    t_spec = pl.BlockSpec((None, N_ATTN_HEADS, HEAD_DIM, tm), lambda b, i: (b, 0, 0, i))
    n_spec = pl.BlockSpec((None, N_ATTN_HEADS, tm, HEAD_DIM), lambda b, i: (b, 0, i, 0))
    return pl.pallas_call(
        _inproj_kernel,
        grid=(B, S // tm),
        in_specs=[
            pl.BlockSpec((None, tm, D), lambda b, i: (b, i, 0)),
            pl.BlockSpec((1, D), lambda b, i: (0, 0)),
            pl.BlockSpec(w_bf16.shape, lambda b, i: (0, 0), pipeline_mode=pl.Buffered(1)),
        ],
        out_specs=[t_spec, n_spec, t_spec,
                   pl.BlockSpec((None, tm, CONV_CH), lambda b, i: (b, i, 0))],
        out_shape=[head_t, head_n, head_t, hg],
        compiler_params=_params(("arbitrary", "arbitrary")),
        name="inproj",
    )(x, g, w_bf16)


def _bias_kernel(rb_ref, out_ref):
    h = pl.program_id(0)
    koff = lax.broadcasted_iota(jnp.int32, (MOBA_BLOCK, MOBA_BLOCK), 0)
    qoff = lax.broadcasted_iota(jnp.int32, (MOBA_BLOCK, MOBA_BLOCK), 1)
    max_exact = N_BUCKETS // 2
    for delta in range(N_BIAS_TILES):
        dist = jnp.maximum(delta * MOBA_BLOCK + qoff - koff, 0)
        nf = jnp.maximum(dist, max_exact).astype(F32)
        large = max_exact + (jnp.log(nf / max_exact) / math.log(MAX_DISTANCE / max_exact)
                             * (N_BUCKETS - max_exact)).astype(jnp.int32)
        large = jnp.minimum(large, N_BUCKETS - 1)
        bucket = jnp.where(dist < max_exact, dist, large)
        val = jnp.zeros((MOBA_BLOCK, MOBA_BLOCK), F32)
        for b in range(N_BUCKETS):
            val = jnp.where(bucket == b, rb_ref[b, h], val)
        if delta == 0:
            val = jnp.where(koff <= qoff, val, NEG_INF)
        out_ref[delta] = (val * LOG2E).astype(out_ref.dtype)


def _bias_tiles(rel_bias):
    return pl.pallas_call(
        _bias_kernel,
        grid=(N_ATTN_HEADS,),
        in_specs=[pl.BlockSpec(memory_space=pltpu.SMEM)],
        out_specs=pl.BlockSpec((None, N_BIAS_TILES, MOBA_BLOCK, MOBA_BLOCK),
                               lambda h: (h, 0, 0, 0)),
        out_shape=jax.ShapeDtypeStruct(
            (N_ATTN_HEADS, N_BIAS_TILES, MOBA_BLOCK, MOBA_BLOCK), F32),
        compiler_params=_params(("arbitrary",)),
        name="bias_tiles",
    )(rel_bias)


def _moba_kernel(qT_ref, k_ref, vT_ref, bias_ref, o_ref,
                 kbh_scr, kbl_scr, sel_scr, s_scr, p_scr):
    nb = kbh_scr.shape[0]

    def mean_body(n, carry):
        blk = k_ref[pl.ds(pl.multiple_of(n * MOBA_BLOCK, MOBA_BLOCK), MOBA_BLOCK), :]
        kbar = jnp.sum(blk.astype(F32), axis=0, keepdims=True) * (1.0 / MOBA_BLOCK)
        hi = kbar.astype(BF16)
        kbh_scr[pl.ds(n, 1), :] = hi.astype(F32)
        kbl_scr[pl.ds(n, 1), :] = kbar - hi.astype(F32)
        return carry
    lax.fori_loop(0, nb, mean_body, 0)

    def tile_body(qi, carry):
        _moba_tile(qi, qT_ref, k_ref, vT_ref, bias_ref, o_ref,
                   kbh_scr, kbl_scr, sel_scr, s_scr, p_scr)
        return carry
    lax.fori_loop(0, nb, tile_body, 0)


def _moba_tile(qi, qT_ref, k_ref, vT_ref, bias_ref, o_ref,
               kbh_scr, kbl_scr, sel_scr, s_scr, p_scr):
    nb = kbh_scr.shape[0]
    q0 = pl.multiple_of(qi * MOBA_BLOCK, MOBA_BLOCK)
    qT = qT_ref[:, pl.ds(q0, MOBA_BLOCK)]

    gate = (jnp.dot(kbh_scr[...].astype(BF16), qT, preferred_element_type=F32)
            + jnp.dot(kbl_scr[...].astype(BF16), qT, preferred_element_type=F32))
    blk_id = lax.broadcasted_iota(jnp.int32, gate.shape, 0)
    gate = jnp.where(blk_id < qi, gate, NEG_INF)
    selected = blk_id == qi
    for _ in range(MOBA_TOPK):
        mx = jnp.max(gate, axis=0, keepdims=True)
        is_max = jnp.logical_and(gate == mx, mx > NEG_INF)
        first = jnp.min(jnp.where(is_max, blk_id, nb), axis=0, keepdims=True)
        pick = blk_id == first
        selected = jnp.logical_or(selected, pick)
        gate = jnp.where(pick, NEG_INF, gate)
    sel_scr[...] = jnp.where(selected, 0.0, NEG_INF)

    span = 2 * MOBA_BLOCK

    def group_blocks(g):
        top = qi - 2 * g
        lo = jnp.clip(top - 1, 0, nb - 2)
        return top, lo

    def mask_row(top, blk):
        row = sel_scr[pl.ds(blk, 1), :]
        return jnp.where(blk <= top, row, NEG_INF)

    def score(g):
        _, lo = group_blocks(g)
        r0 = pl.multiple_of(lo * MOBA_BLOCK, MOBA_BLOCK)
        return jnp.dot(k_ref[pl.ds(r0, span), :], qT, preferred_element_type=F32)

    def prep(s_all, g, slot):
        top, lo = group_blocks(g)
        mx = None
        for t in range(2):
            blk = lo + t
            rows = slice(t * MOBA_BLOCK, (t + 1) * MOBA_BLOCK)
            x = s_all[rows] + bias_ref[jnp.clip(qi - blk, 0, N_BIAS_TILES - 1)]
            s_scr[slot, rows, :] = x
            bmx = jnp.max(x, axis=0, keepdims=True) + mask_row(top, blk)
            mx = bmx if mx is None else jnp.maximum(mx, bmx)
        return mx

    def weighted_values(g, slot):
        _, lo = group_blocks(g)
        r0 = pl.multiple_of(lo * MOBA_BLOCK, MOBA_BLOCK)
        return jnp.dot(vT_ref[:, pl.ds(r0, span)], p_scr[slot], preferred_element_type=F32)

    def softmax(g, slot, m, l, mx):
        top, lo = group_blocks(g)
        m_new = jnp.maximum(m, mx)
        alpha = jnp.exp2(m - m_new)
        psum = None
        for t in range(2):
            rows = slice(t * MOBA_BLOCK, (t + 1) * MOBA_BLOCK)
            shift = m_new - mask_row(top, lo + t)
            p = jnp.exp2(s_scr[slot, rows, :] - shift)
            p_scr[slot, rows, :] = p.astype(BF16)
            psum = p if psum is None else psum + p
        return m_new, alpha * l + jnp.sum(psum, axis=0, keepdims=True), alpha

    slots = range(TRIP_GROUPS)

    def accumulate(acc, g, alphas):
        pvs = [weighted_values(g + i, i) for i in slots]
        for a, pv in zip(alphas, pvs):
            acc = a * acc + pv
        return acc

    def body(t, carry):
        m, l, acc, mxs, alphas = carry
        g = TRIP_GROUPS * t
        scores = [score(g + TRIP_GROUPS + i) for i in slots]
        acc = accumulate(acc, g - TRIP_GROUPS, alphas)
        new_alphas = []
        for i in slots:
            m, l, a = softmax(g + i, i, m, l, mxs[i])
            new_alphas.append(a)
        new_mxs = [prep(scores[i], g + TRIP_GROUPS + i, i) for i in slots]
        return m, l, acc, tuple(new_mxs), tuple(new_alphas)

    p_scr[...] = jnp.zeros(p_scr.shape, BF16)
    n_groups = qi // 2 + 1
    n_trips = (n_groups + TRIP_GROUPS - 1) // TRIP_GROUPS
    row = lambda v: jnp.full((1, MOBA_BLOCK), v, F32)
    init = (row(NEG_INF), row(0.0), jnp.zeros((HEAD_DIM, MOBA_BLOCK), F32),
            tuple(prep(score(i), i, i) for i in slots), tuple(row(1.0) for _ in slots))
    m, l, acc, _, alphas = lax.fori_loop(0, n_trips, body, init)
    acc = accumulate(acc, TRIP_GROUPS * (n_trips - 1), alphas)
    o_ref[pl.ds(q0, MOBA_BLOCK), :] = (acc / l).T.astype(o_ref.dtype)


def _moba(qT, k, vT, bias):
    B, H, _, S = qT.shape
    nb = S // MOBA_BLOCK
    return pl.pallas_call(
        _moba_kernel,
        grid=(B, H),
        in_specs=[
            pl.BlockSpec((None, None, HEAD_DIM, S), lambda b, h: (b, h, 0, 0)),
            pl.BlockSpec((None, None, S, HEAD_DIM), lambda b, h: (b, h, 0, 0)),
            pl.BlockSpec((None, None, HEAD_DIM, S), lambda b, h: (b, h, 0, 0)),
            pl.BlockSpec((None, N_BIAS_TILES, MOBA_BLOCK, MOBA_BLOCK),
                         lambda b, h: (h, 0, 0, 0)),
        ],
        out_specs=pl.BlockSpec((None, S, HEAD_DIM), lambda b, h: (b, 0, h)),
        out_shape=jax.ShapeDtypeStruct((B, S, ATTN_WIDTH), BF16),
        scratch_shapes=[pltpu.VMEM((nb, HEAD_DIM), F32), pltpu.VMEM((nb, HEAD_DIM), F32),
                        pltpu.VMEM((nb, MOBA_BLOCK), F32),
                        pltpu.VMEM((TRIP_GROUPS, 2 * MOBA_BLOCK, MOBA_BLOCK), F32),
                        pltpu.VMEM((TRIP_GROUPS, 2 * MOBA_BLOCK, MOBA_BLOCK), BF16)],
        compiler_params=_params(("arbitrary", "arbitrary")),
        name="moba",
    )(qT, k, vT, bias)


def _conv_kernel(cur_ref, halo_ref, w_ref, b_ref, g_ref, beta_ref, o_ref, xs_scr, y_scr):
    i = pl.program_id(1)
    ts = cur_ref.shape[0]
    rc = 32
    n_rows = ts + CONV_HALO
    n_shift_rows = n_rows - SUBLANES
    first_tap = CONV_HALO - (CONV_K - 1)
    for c in range(CONV_CH // 128):
        lanes = slice(c * 128, (c + 1) * 128)
        xs_scr[c, 0, 0:CONV_HALO, :] = jnp.where(i == 0, 0.0, halo_ref[:, lanes])
        xs_scr[c, 0, CONV_HALO:n_rows, :] = cur_ref[:, lanes]
        for s in range(1, SUBLANES):
            xs_scr[c, s, 0:n_shift_rows, :] = xs_scr[c, 0, s:s + n_shift_rows, :]
        w_tiles = [jnp.broadcast_to(w_ref[t:t + 1, lanes], (SUBLANES, 128))
                   for t in range(CONV_K)]
        b_tile = jnp.broadcast_to(b_ref[:, lanes], (SUBLANES, 128))

        def rows_body(r, carry):
            for u in range(rc // SUBLANES):
                row0 = pl.multiple_of(r * rc + u * SUBLANES, SUBLANES)
                parts = [b_tile] + [None] * (CONV_CHAINS - 1)
                for t in range(CONV_K):
                    s, a = (first_tap + t) % SUBLANES, (first_tap + t) // SUBLANES
                    term = w_tiles[t] * xs_scr[c, s, pl.ds(row0 + SUBLANES * a, SUBLANES), :]
                    i_part = t % CONV_CHAINS
                    parts[i_part] = term if parts[i_part] is None else parts[i_part] + term
                while len(parts) > 1:
                    parts = [parts[i] + parts[i + 1] for i in range(0, len(parts), 2)]
                y_scr[pl.ds(row0, SUBLANES), lanes] = parts[0]
            return carry

        lax.fori_loop(0, ts // rc, rows_body, 0)
    y = y_scr[...]
    mu = jnp.mean(y, axis=-1, keepdims=True)
    yc = y - mu
    var = jnp.mean(yc * yc, axis=-1, keepdims=True)
    z = yc * lax.rsqrt(var + EPS) * g_ref[...] + beta_ref[...]
    o_ref[...] = (z * jax.nn.sigmoid(z)).astype(o_ref.dtype)


def _odd_tiles(rows):
    tiles = -(-rows // SUBLANES)
    return (tiles + 1 - tiles % 2) * SUBLANES


def _conv_branch(hg, conv_w, conv_b, ln_g, ln_b, ts=256):
    B, S, C = hg.shape
    halo_blocks = ts // CONV_HALO
    vec = pl.BlockSpec((1, C), lambda b, i: (0, 0))
    return pl.pallas_call(
        _conv_kernel,
        grid=(B, S // ts),
        in_specs=[
            pl.BlockSpec((None, ts, C), lambda b, i: (b, i, 0)),
            pl.BlockSpec((None, CONV_HALO, C),
                         lambda b, i: (b, jnp.maximum(i * halo_blocks - 1, 0), 0)),
            pl.BlockSpec((CONV_K, C), lambda b, i: (0, 0)),
            vec, vec, vec,
        ],
        out_specs=pl.BlockSpec((None, ts, C), lambda b, i: (b, i, 0)),
        out_shape=jax.ShapeDtypeStruct((B, S, C), BF16),
        scratch_shapes=[pltpu.VMEM((C // 128, SUBLANES, _odd_tiles(ts + CONV_HALO), 128), F32),
                        pltpu.VMEM((ts, C), F32)],
        compiler_params=_params(("arbitrary", "arbitrary")),
        name="conv_branch",
    )(hg, hg, conv_w, conv_b, ln_g, ln_b)


def _memkv_kernel(m_ref, g_ref, wk_ref, wv_ref, k_ref, v_ref):
    n = _rms(m_ref[...], g_ref[...]).astype(BF16)
    k_ref[...] = jnp.dot(n, wk_ref[...], preferred_element_type=F32).astype(BF16)
    v_ref[...] = jnp.dot(n, wv_ref[...], preferred_element_type=F32).astype(BF16)


def _memkv(mem2, g, wk, wv):
    M, D = mem2.shape
    full = lambda shape: pl.BlockSpec(shape, lambda i: (0, 0))
    out = jax.ShapeDtypeStruct((M, CROSS_WIDTH), BF16)
    return pl.pallas_call(
        _memkv_kernel,
        grid=(1,),
        in_specs=[full((M, D)), full((1, D)), full((D, CROSS_WIDTH)), full((D, CROSS_WIDTH))],
        out_specs=[full((M, CROSS_WIDTH)), full((M, CROSS_WIDTH))],
        out_shape=[out, out],
        compiler_params=_params(("arbitrary",)),
        name="mem_kv",
    )(mem2, g, wk, wv)


def _mix_cross_kernel(x_ref, a_ref, c_ref, wout_ref, g_ref, wq_ref, kc_ref, vc_ref, wo_ref,
                      o_ref):
    scale = CROSS_HEAD_DIM ** -0.5
    half = x_ref.shape[0] // ROW_HALVES
    for r in range(ROW_HALVES):
        rows = slice(r * half, (r + 1) * half)
        h1 = (x_ref[rows, :]
              + jnp.dot(a_ref[rows, :], wout_ref[0:ATTN_WIDTH, :], preferred_element_type=F32)
              + jnp.dot(c_ref[rows, :], wout_ref[ATTN_WIDTH:, :], preferred_element_type=F32))
        n = _rms(h1, g_ref[...]).astype(BF16)
        q = jnp.dot(n, wq_ref[...], preferred_element_type=F32).astype(BF16)
        outs = []
        for hd in range(N_CROSS_HEADS):
            cols = slice(hd * CROSS_HEAD_DIM, (hd + 1) * CROSS_HEAD_DIM)
            s = lax.dot_general(q[:, cols], kc_ref[:, cols], (((1,), (1,)), ((), ())),
                                preferred_element_type=F32) * scale
            m = jnp.max(s, axis=-1, keepdims=True)
            p = jnp.exp(s - m)
            p = p / jnp.sum(p, axis=-1, keepdims=True)
            outs.append(jnp.dot(p.astype(BF16), vc_ref[:, cols], preferred_element_type=F32))
        o = jnp.concatenate(outs, axis=-1).astype(BF16)
        o_ref[rows, :] = h1 + jnp.dot(o, wo_ref[...], preferred_element_type=F32)


def _mix_cross(x, a, c, w_out, g, wq, kc, vc, wo, tm=512):
    B, S, D = x.shape
    n_mem = kc.shape[1]
    resident = lambda arr: pl.BlockSpec(arr.shape, lambda b, i: (0,) * arr.ndim,
                                        pipeline_mode=pl.Buffered(1))
    rows = lambda width: pl.BlockSpec((None, tm, width), lambda b, i: (b, i, 0))
    per_batch = pl.BlockSpec((None, n_mem, CROSS_WIDTH), lambda b, i: (b, 0, 0))
    return pl.pallas_call(
        _mix_cross_kernel,
        grid=(B, S // tm),
        in_specs=[rows(D), rows(ATTN_WIDTH), rows(CONV_CH), resident(w_out), resident(g),
                  resident(wq), per_batch, per_batch, resident(wo)],
        out_specs=rows(D),
        out_shape=jax.ShapeDtypeStruct((B, S, D), F32),
        compiler_params=_params(("arbitrary", "arbitrary")),
        name="mix_cross",
    )(x, a, c, w_out, g, wq, kc, vc, wo)


def _mlp_kernel(h_ref, g_ref, w1_ref, w2_ref, gf_ref, o_ref, n_scr, acc_scr):
    f = pl.program_id(1)

    @pl.when(f == 0)
    def _():
        h = h_ref[...]
        n_scr[...] = _rms(h, g_ref[...]).astype(BF16)
        acc_scr[...] = h

    u = jnp.dot(n_scr[...], w1_ref[...], preferred_element_type=F32)
    u = jnp.square(jnp.maximum(u, 0.0)).astype(BF16)
    acc_scr[...] += jnp.dot(u, w2_ref[...], preferred_element_type=F32)

    @pl.when(f == pl.num_programs(1) - 1)
    def _():
        o_ref[...] = _rms(acc_scr[...], gf_ref[...])


def _mlp(h2, g, w1, w2, g_final, tm=512, tf=1024):
    M, D = h2.shape
    vec = pl.BlockSpec((1, D), lambda i, f: (0, 0))
    return pl.pallas_call(
        _mlp_kernel,
        grid=(M // tm, D_FF // tf),
        in_specs=[
            pl.BlockSpec((tm, D), lambda i, f: (i, 0)),
            vec,
            pl.BlockSpec((D, tf), lambda i, f: (0, f)),
            pl.BlockSpec((tf, D), lambda i, f: (f, 0)),
            vec,
        ],
        out_specs=pl.BlockSpec((tm, D), lambda i, f: (i, 0)),
        out_shape=jax.ShapeDtypeStruct((M, D), F32),
        scratch_shapes=[pltpu.VMEM((tm, D), BF16), pltpu.VMEM((tm, D), F32)],
        compiler_params=_params(("arbitrary", "arbitrary")),
        name="mlp",
    )(h2, g, w1, w2, g_final)


@jax.jit
def _layer(x, mem, g_mix, w_in, conv_w, conv_b, conv_ln_g, conv_ln_b, w_out, rel_bias,
           g_cross, g_mem, wq_c, wk_c, wv_c, wo_c, g_mlp, w1, w2, g_final):
    B, S, D = x.shape
    row = lambda v: v.reshape(1, -1)
    bias = _bias_tiles(rel_bias)
    qT, k, vT, hg = _inproj(x, row(g_mix), w_in.astype(BF16))
    a = _moba(qT, k, vT, bias)
    c = _conv_branch(hg, conv_w, row(conv_b), row(conv_ln_g), row(conv_ln_b))
    kc, vc = _memkv(mem.reshape(-1, D), row(g_mem), wk_c.astype(BF16), wv_c.astype(BF16))
    n_mem = mem.shape[1]
    h2 = _mix_cross(x, a, c, w_out.astype(BF16), row(g_cross), wq_c.astype(BF16),
                    kc.reshape(B, n_mem, CROSS_WIDTH), vc.reshape(B, n_mem, CROSS_WIDTH),
                    wo_c.astype(BF16))
    out = _mlp(h2.reshape(B * S, D), row(g_mlp), w1.astype(BF16), w2.astype(BF16),
               row(g_final))
    return out.reshape(B, S, D)


def kernel(x, mem, g_mix, w_in, conv_w, conv_b, conv_ln_g, conv_ln_b, w_out, rel_bias,
           g_cross, g_mem, wq_c, wk_c, wv_c, wo_c, g_mlp, w1, w2, g_final):
    return _layer(x, mem, g_mix[0], w_in[0], conv_w[0], conv_b[0], conv_ln_g[0], conv_ln_b[0],
                  w_out[0], rel_bias, g_cross[0], g_mem[0], wq_c[0], wk_c[0], wv_c[0],
                  wo_c[0], g_mlp[0], w1[0], w2[0], g_final)
```

```python
import functools
import math

import jax
import jax.numpy as jnp
from jax import lax
from jax.experimental import pallas as pl
from jax.experimental.pallas import tpu as pltpu

D_MODEL = 2048
HEAD_DIM = 128
N_ATTN_HEADS = 8
ATTN_WIDTH = N_ATTN_HEADS * HEAD_DIM
CONV_CH = D_MODEL // 2
CONV_K = 31
MOBA_BLOCK = 256
MOBA_TOPK = 3
N_BUCKETS = 32
MAX_DISTANCE = 2048
N_CROSS_HEADS = 4
CROSS_HEAD_DIM = 128
CROSS_WIDTH = N_CROSS_HEADS * CROSS_HEAD_DIM
D_FF = 4 * D_MODEL
EPS = 1e-6

F32 = jnp.float32
BF16 = jnp.bfloat16
NEG_INF = float("-inf")
LOG2E = math.log2(math.e)
Q_PRESCALE = HEAD_DIM ** -0.5 * LOG2E

N_BIAS_TILES = 8
SUBLANES = 8
TRIP_GROUPS = 2
ROW_HALVES = 2
CONV_CHAINS = 4
CONV_HALO = 32

V7X_VMEM_LIMIT = 56 * 1024 * 1024


def _params(sem):
    return pltpu.CompilerParams(dimension_semantics=sem, vmem_limit_bytes=V7X_VMEM_LIMIT)


def _rms(xf, g):
    return xf * lax.rsqrt(jnp.mean(xf * xf, axis=-1, keepdims=True) + EPS) * g


def _inproj_kernel(x_ref, g_ref, w_ref, qT_ref, k_ref, vT_ref, hg_ref):
    n = _rms(x_ref[...], g_ref[...]).astype(BF16)

    def proj(col):
        return jnp.dot(n, w_ref[:, col * ATTN_WIDTH:(col + 1) * ATTN_WIDTH],
                       preferred_element_type=F32)

    def heads(z):
        return [z[:, h * HEAD_DIM:(h + 1) * HEAD_DIM] for h in range(N_ATTN_HEADS)]

    for h, zh in enumerate(heads(proj(0))):
        qT_ref[h] = (zh * Q_PRESCALE).T.astype(BF16)
    for h, zh in enumerate(heads(proj(1))):
        k_ref[h] = zh.astype(BF16)
    for h, zh in enumerate(heads(proj(2))):
        vT_ref[h] = zh.T.astype(BF16)
    hg_ref[...] = proj(3) * jax.nn.sigmoid(proj(4))


def _inproj(x, g, w_bf16, tm=512):
    B, S, D = x.shape
    head_t = jax.ShapeDtypeStruct((B, N_ATTN_HEADS, HEAD_DIM, S), BF16)
    head_n = jax.ShapeDtypeStruct((B, N_ATTN_HEADS, S, HEAD_DIM), BF16)
    hg = jax.ShapeDtypeStruct((B, S, CONV_CH), F32)
    t_spec = pl.BlockSpec((None, N_ATTN_HEADS, HEAD_DIM, tm), lambda b, i: (b, 0, 0, i))
    n_spec = pl.BlockSpec((None, N_ATTN_HEADS, tm, HEAD_DIM), lambda b, i: (b, 0, i, 0))
    return pl.pallas_call(
        _inproj_kernel,
        grid=(B, S // tm),
        in_specs=[
            pl.BlockSpec((None, tm, D), lambda b, i: (b, i, 0)),
            pl.BlockSpec((1, D), lambda b, i: (0, 0)),
            pl.BlockSpec(w_bf16.shape, lambda b, i: (0, 0), pipeline_mode=pl.Buffered(1)),
        ],
        out_specs=[t_spec, n_spec, t_spec,
                   pl.BlockSpec((None, tm, CONV_CH), lambda b, i: (b, i, 0))],
        out_shape=[head_t, head_n, head_t, hg],
        compiler_params=_params(("arbitrary", "arbitrary")),
        name="inproj",
    )(x, g, w_bf16)


def _bias_kernel(rb_ref, out_ref):
    h = pl.program_id(0)
    koff = lax.broadcasted_iota(jnp.int32, (MOBA_BLOCK, MOBA_BLOCK), 0)
    qoff = lax.broadcasted_iota(jnp.int32, (MOBA_BLOCK, MOBA_BLOCK), 1)
    max_exact = N_BUCKETS // 2
    for delta in range(N_BIAS_TILES):
        dist = jnp.maximum(delta * MOBA_BLOCK + qoff - koff, 0)
        nf = jnp.maximum(dist, max_exact).astype(F32)
        large = max_exact + (jnp.log(nf / max_exact) / math.log(MAX_DISTANCE / max_exact)
                             * (N_BUCKETS - max_exact)).astype(jnp.int32)
        large = jnp.minimum(large, N_BUCKETS - 1)
        bucket = jnp.where(dist < max_exact, dist, large)
        val = jnp.zeros((MOBA_BLOCK, MOBA_BLOCK), F32)
        for b in range(N_BUCKETS):
            val = jnp.where(bucket == b, rb_ref[b, h], val)
        if delta == 0:
            val = jnp.where(koff <= qoff, val, NEG_INF)
        out_ref[delta] = (val * LOG2E).astype(out_ref.dtype)


def _bias_tiles(rel_bias):
    return pl.pallas_call(
        _bias_kernel,
        grid=(N_ATTN_HEADS,),
        in_specs=[pl.BlockSpec(memory_space=pltpu.SMEM)],
        out_specs=pl.BlockSpec((None, N_BIAS_TILES, MOBA_BLOCK, MOBA_BLOCK),
                               lambda h: (h, 0, 0, 0)),
        out_shape=jax.ShapeDtypeStruct(
            (N_ATTN_HEADS, N_BIAS_TILES, MOBA_BLOCK, MOBA_BLOCK), F32),
        compiler_params=_params(("arbitrary",)),
        name="bias_tiles",
    )(rel_bias)


def _moba_kernel(qT_ref, k_ref, vT_ref, bias_ref, o_ref,
                 kbh_scr, kbl_scr, sel_scr, s_scr, p_scr):
    qi = pl.program_id(2)
    nb = kbh_scr.shape[0]

    @pl.when(qi == 0)
    def _():
        def mean_body(n, carry):
            blk = k_ref[pl.ds(pl.multiple_of(n * MOBA_BLOCK, MOBA_BLOCK), MOBA_BLOCK), :]
            kbar = jnp.sum(blk.astype(F32), axis=0, keepdims=True) * (1.0 / MOBA_BLOCK)
            hi = kbar.astype(BF16)
            kbh_scr[pl.ds(n, 1), :] = hi.astype(F32)
            kbl_scr[pl.ds(n, 1), :] = kbar - hi.astype(F32)
            return carry
        lax.fori_loop(0, nb, mean_body, 0)

    qT = qT_ref[...]

    gate = (jnp.dot(kbh_scr[...].astype(BF16), qT, preferred_element_type=F32)
            + jnp.dot(kbl_scr[...].astype(BF16), qT, preferred_element_type=F32))
    blk_id = lax.broadcasted_iota(jnp.int32, gate.shape, 0)
    gate = jnp.where(blk_id < qi, gate, NEG_INF)
    selected = blk_id == qi
    for _ in range(MOBA_TOPK):
        mx = jnp.max(gate, axis=0, keepdims=True)
        is_max = jnp.logical_and(gate == mx, mx > NEG_INF)
        first = jnp.min(jnp.where(is_max, blk_id, nb), axis=0, keepdims=True)
        pick = blk_id == first
        selected = jnp.logical_or(selected, pick)
        gate = jnp.where(pick, NEG_INF, gate)
    sel_scr[...] = jnp.where(selected, 0.0, NEG_INF)

    span = 2 * MOBA_BLOCK

    def group_blocks(g):
        top = qi - 2 * g
        lo = jnp.clip(top - 1, 0, nb - 2)
        return top, lo

    def mask_row(top, blk):
        row = sel_scr[pl.ds(blk, 1), :]
        return jnp.where(blk <= top, row, NEG_INF)

    def score(g):
        _, lo = group_blocks(g)
        r0 = pl.multiple_of(lo * MOBA_BLOCK, MOBA_BLOCK)
        return jnp.dot(k_ref[pl.ds(r0, span), :], qT, preferred_element_type=F32)

    def prep(s_all, g, slot):
        top, lo = group_blocks(g)
        mx = None
        for t in range(2):
            blk = lo + t
            rows = slice(t * MOBA_BLOCK, (t + 1) * MOBA_BLOCK)
            x = s_all[rows] + bias_ref[jnp.clip(qi - blk, 0, N_BIAS_TILES - 1)]
            s_scr[slot, rows, :] = x
            bmx = jnp.max(x, axis=0, keepdims=True) + mask_row(top, blk)
            mx = bmx if mx is None else jnp.maximum(mx, bmx)
        return mx

    def weighted_values(g, slot):
        _, lo = group_blocks(g)
        r0 = pl.multiple_of(lo * MOBA_BLOCK, MOBA_BLOCK)
        return jnp.dot(vT_ref[:, pl.ds(r0, span)], p_scr[slot], preferred_element_type=F32)

    def softmax(g, slot, m, l, mx):
        top, lo = group_blocks(g)
        m_new = jnp.maximum(m, mx)
        alpha = jnp.exp2(m - m_new)
        psum = None
        for t in range(2):
            rows = slice(t * MOBA_BLOCK, (t + 1) * MOBA_BLOCK)
            shift = m_new - mask_row(top, lo + t)
            p = jnp.exp2(s_scr[slot, rows, :] - shift)
            p_scr[slot, rows, :] = p.astype(BF16)
            psum = p if psum is None else psum + p
        return m_new, alpha * l + jnp.sum(psum, axis=0, keepdims=True), alpha

    slots = range(TRIP_GROUPS)

    def accumulate(acc, g, alphas):
        pvs = [weighted_values(g + i, i) for i in slots]
        for a, pv in zip(alphas, pvs):
            acc = a * acc + pv
        return acc

    def body(t, carry):
        m, l, acc, mxs, alphas = carry
        g = TRIP_GROUPS * t
        scores = [score(g + TRIP_GROUPS + i) for i in slots]
        acc = accumulate(acc, g - TRIP_GROUPS, alphas)
        new_alphas = []
        for i in slots:
            m, l, a = softmax(g + i, i, m, l, mxs[i])
            new_alphas.append(a)
        new_mxs = [prep(scores[i], g + TRIP_GROUPS + i, i) for i in slots]
        return m, l, acc, tuple(new_mxs), tuple(new_alphas)

    p_scr[...] = jnp.zeros(p_scr.shape, BF16)
    n_groups = qi // 2 + 1
    n_trips = (n_groups + TRIP_GROUPS - 1) // TRIP_GROUPS
    row = lambda v: jnp.full((1, MOBA_BLOCK), v, F32)
    init = (row(NEG_INF), row(0.0), jnp.zeros((HEAD_DIM, MOBA_BLOCK), F32),
            tuple(prep(score(i), i, i) for i in slots), tuple(row(1.0) for _ in slots))
    m, l, acc, mxs, alphas = lax.fori_loop(0, n_trips - 1, body, init)
    g = TRIP_GROUPS * (n_trips - 1)
    acc = accumulate(acc, g - TRIP_GROUPS, alphas)
    last_alphas = []
    for i in slots:
        m, l, a = softmax(g + i, i, m, l, mxs[i])
        last_alphas.append(a)
    acc = accumulate(acc, g, last_alphas)
    o_ref[...] = (acc / l).T.astype(o_ref.dtype)


def _moba(qT, k, vT, bias):
    B, H, _, S = qT.shape
    nb = S // MOBA_BLOCK
    return pl.pallas_call(
        _moba_kernel,
        grid=(B, H, nb),
        in_specs=[
            pl.BlockSpec((None, None, HEAD_DIM, MOBA_BLOCK), lambda b, h, i: (b, h, 0, i)),
            pl.BlockSpec((None, None, S, HEAD_DIM), lambda b, h, i: (b, h, 0, 0)),
            pl.BlockSpec((None, None, HEAD_DIM, S), lambda b, h, i: (b, h, 0, 0)),
            pl.BlockSpec((None, N_BIAS_TILES, MOBA_BLOCK, MOBA_BLOCK),
                         lambda b, h, i: (h, 0, 0, 0)),
        ],
        out_specs=pl.BlockSpec((None, MOBA_BLOCK, HEAD_DIM), lambda b, h, i: (b, i, h)),
        out_shape=jax.ShapeDtypeStruct((B, S, ATTN_WIDTH), BF16),
        scratch_shapes=[pltpu.VMEM((nb, HEAD_DIM), F32), pltpu.VMEM((nb, HEAD_DIM), F32),
                        pltpu.VMEM((nb, MOBA_BLOCK), F32),
                        pltpu.VMEM((TRIP_GROUPS, 2 * MOBA_BLOCK, MOBA_BLOCK), F32),
                        pltpu.VMEM((TRIP_GROUPS, 2 * MOBA_BLOCK, MOBA_BLOCK), BF16)],
        compiler_params=_params(("arbitrary", "arbitrary", "arbitrary")),
        name="moba",
    )(qT, k, vT, bias)


def _conv_kernel(cur_ref, halo_ref, w_ref, b_ref, g_ref, beta_ref, o_ref, xs_scr, y_scr):
    i = pl.program_id(1)
    ts = cur_ref.shape[0]
    rc = 32
    n_rows = ts + CONV_HALO
    n_shift_rows = n_rows - SUBLANES
    first_tap = CONV_HALO - (CONV_K - 1)
    for c in range(CONV_CH // 128):
        lanes = slice(c * 128, (c + 1) * 128)
        xs_scr[c, 0, 0:CONV_HALO, :] = jnp.where(i == 0, 0.0, halo_ref[:, lanes])
        xs_scr[c, 0, CONV_HALO:n_rows, :] = cur_ref[:, lanes]
        for s in range(1, SUBLANES):
            xs_scr[c, s, 0:n_shift_rows, :] = xs_scr[c, 0, s:s + n_shift_rows, :]
        w_tiles = [jnp.broadcast_to(w_ref[t:t + 1, lanes], (SUBLANES, 128))
                   for t in range(CONV_K)]
        b_tile = jnp.broadcast_to(b_ref[:, lanes], (SUBLANES, 128))

        def rows_body(r, carry):
            for u in range(rc // SUBLANES):
                row0 = pl.multiple_of(r * rc + u * SUBLANES, SUBLANES)
                parts = [b_tile] + [None] * (CONV_CHAINS - 1)
                for t in range(CONV_K):
                    s, a = (first_tap + t) % SUBLANES, (first_tap + t) // SUBLANES
                    term = w_tiles[t] * xs_scr[c, s, pl.ds(row0 + SUBLANES * a, SUBLANES), :]
                    i_part = t % CONV_CHAINS
                    parts[i_part] = term if parts[i_part] is None else parts[i_part] + term
                while len(parts) > 1:
                    parts = [parts[i] + parts[i + 1] for i in range(0, len(parts), 2)]
                y_scr[pl.ds(row0, SUBLANES), lanes] = parts[0]
            return carry

        lax.fori_loop(0, ts // rc, rows_body, 0)
    y = y_scr[...]
    mu = jnp.mean(y, axis=-1, keepdims=True)
    yc = y - mu
    var = jnp.mean(yc * yc, axis=-1, keepdims=True)
    z = yc * lax.rsqrt(var + EPS) * g_ref[...] + beta_ref[...]
    o_ref[...] = (z * jax.nn.sigmoid(z)).astype(o_ref.dtype)


def _odd_tiles(rows):
    tiles = -(-rows // SUBLANES)
    return (tiles + 1 - tiles % 2) * SUBLANES


def _conv_branch(hg, conv_w, conv_b, ln_g, ln_b, ts=256):
    B, S, C = hg.shape
    halo_blocks = ts // CONV_HALO
    vec = pl.BlockSpec((1, C), lambda b, i: (0, 0))
    return pl.pallas_call(
        _conv_kernel,
        grid=(B, S // ts),
        in_specs=[
            pl.BlockSpec((None, ts, C), lambda b, i: (b, i, 0)),
            pl.BlockSpec((None, CONV_HALO, C),
                         lambda b, i: (b, jnp.maximum(i * halo_blocks - 1, 0), 0)),
            pl.BlockSpec((CONV_K, C), lambda b, i: (0, 0)),
            vec, vec, vec,
        ],
        out_specs=pl.BlockSpec((None, ts, C), lambda b, i: (b, i, 0)),
        out_shape=jax.ShapeDtypeStruct((B, S, C), BF16),
        scratch_shapes=[pltpu.VMEM((C // 128, SUBLANES, _odd_tiles(ts + CONV_HALO), 128), F32),
                        pltpu.VMEM((ts, C), F32)],
        compiler_params=_params(("arbitrary", "arbitrary")),
        name="conv_branch",
    )(hg, hg, conv_w, conv_b, ln_g, ln_b)


def _memkv_kernel(m_ref, g_ref, wk_ref, wv_ref, k_ref, v_ref):
    n = _rms(m_ref[...], g_ref[...]).astype(BF16)
    k_ref[...] = jnp.dot(n, wk_ref[...], preferred_element_type=F32).astype(BF16)
    v_ref[...] = jnp.dot(n, wv_ref[...], preferred_element_type=F32).astype(BF16)


def _memkv(mem2, g, wk, wv):
    M, D = mem2.shape
    full = lambda shape: pl.BlockSpec(shape, lambda i: (0, 0))
    out = jax.ShapeDtypeStruct((M, CROSS_WIDTH), BF16)
    return pl.pallas_call(
        _memkv_kernel,
        grid=(1,),
        in_specs=[full((M, D)), full((1, D)), full((D, CROSS_WIDTH)), full((D, CROSS_WIDTH))],
        out_specs=[full((M, CROSS_WIDTH)), full((M, CROSS_WIDTH))],
        out_shape=[out, out],
        compiler_params=_params(("arbitrary",)),
        name="mem_kv",
    )(mem2, g, wk, wv)


def _mix_cross_kernel(x_ref, a_ref, c_ref, wout_ref, g_ref, wq_ref, kc_ref, vc_ref, wo_ref,
                      o_ref):
    scale = CROSS_HEAD_DIM ** -0.5
    half = x_ref.shape[0] // ROW_HALVES
    for r in range(ROW_HALVES):
        rows = slice(r * half, (r + 1) * half)
        h1 = (x_ref[rows, :]
              + jnp.dot(a_ref[rows, :], wout_ref[0:ATTN_WIDTH, :], preferred_element_type=F32)
              + jnp.dot(c_ref[rows, :], wout_ref[ATTN_WIDTH:, :], preferred_element_type=F32))
        n = _rms(h1, g_ref[...]).astype(BF16)
        q = jnp.dot(n, wq_ref[...], preferred_element_type=F32).astype(BF16)
        outs = []
        for hd in range(N_CROSS_HEADS):
            cols = slice(hd * CROSS_HEAD_DIM, (hd + 1) * CROSS_HEAD_DIM)
            s = lax.dot_general(q[:, cols], kc_ref[:, cols], (((1,), (1,)), ((), ())),
                                preferred_element_type=F32) * scale
            m = jnp.max(s, axis=-1, keepdims=True)
            p = jnp.exp(s - m)
            p = p / jnp.sum(p, axis=-1, keepdims=True)
            outs.append(jnp.dot(p.astype(BF16), vc_ref[:, cols], preferred_element_type=F32))
        o = jnp.concatenate(outs, axis=-1).astype(BF16)
        o_ref[rows, :] = h1 + jnp.dot(o, wo_ref[...], preferred_element_type=F32)


def _mix_cross(x, a, c, w_out, g, wq, kc, vc, wo, tm=512):
    B, S, D = x.shape
    n_mem = kc.shape[1]
    resident = lambda arr: pl.BlockSpec(arr.shape, lambda b, i: (0,) * arr.ndim,
                                        pipeline_mode=pl.Buffered(1))
    rows = lambda width: pl.BlockSpec((None, tm, width), lambda b, i: (b, i, 0))
    per_batch = pl.BlockSpec((None, n_mem, CROSS_WIDTH), lambda b, i: (b, 0, 0))
    return pl.pallas_call(
        _mix_cross_kernel,
        grid=(B, S // tm),
        in_specs=[rows(D), rows(ATTN_WIDTH), rows(CONV_CH), resident(w_out), resident(g),
                  resident(wq), per_batch, per_batch, resident(wo)],
        out_specs=rows(D),
        out_shape=jax.ShapeDtypeStruct((B, S, D), F32),
        compiler_params=_params(("arbitrary", "arbitrary")),
        name="mix_cross",
    )(x, a, c, w_out, g, wq, kc, vc, wo)


def _mlp_kernel(h_ref, g_ref, w1_ref, w2_ref, gf_ref, o_ref, n_scr, acc_scr):
    f = pl.program_id(1)

    @pl.when(f == 0)
    def _():
        h = h_ref[...]
        n_scr[...] = _rms(h, g_ref[...]).astype(BF16)
        acc_scr[...] = h

    u = jnp.dot(n_scr[...], w1_ref[...], preferred_element_type=F32)
    u = jnp.square(jnp.maximum(u, 0.0)).astype(BF16)
    acc_scr[...] += jnp.dot(u, w2_ref[...], preferred_element_type=F32)

    @pl.when(f == pl.num_programs(1) - 1)
    def _():
        o_ref[...] = _rms(acc_scr[...], gf_ref[...])


def _mlp(h2, g, w1, w2, g_final, tm=512, tf=1024):
    M, D = h2.shape
    vec = pl.BlockSpec((1, D), lambda i, f: (0, 0))
    return pl.pallas_call(
        _mlp_kernel,
        grid=(M // tm, D_FF // tf),
        in_specs=[
            pl.BlockSpec((tm, D), lambda i, f: (i, 0)),
            vec,
            pl.BlockSpec((D, tf), lambda i, f: (0, f)),
            pl.BlockSpec((tf, D), lambda i, f: (f, 0)),
            vec,
        ],
        out_specs=pl.BlockSpec((tm, D), lambda i, f: (i, 0)),
        out_shape=jax.ShapeDtypeStruct((M, D), F32),
        scratch_shapes=[pltpu.VMEM((tm, D), BF16), pltpu.VMEM((tm, D), F32)],
        compiler_params=_params(("arbitrary", "arbitrary")),
        name="mlp",
    )(h2, g, w1, w2, g_final)


@jax.jit
def _layer(x, mem, g_mix, w_in, conv_w, conv_b, conv_ln_g, conv_ln_b, w_out, rel_bias,
           g_cross, g_mem, wq_c, wk_c, wv_c, wo_c, g_mlp, w1, w2, g_final):
    B, S, D = x.shape
    row = lambda v: v.reshape(1, -1)
    bias = _bias_tiles(rel_bias)
    qT, k, vT, hg = _inproj(x, row(g_mix), w_in.astype(BF16))
    a = _moba(qT, k, vT, bias)
    c = _conv_branch(hg, conv_w, row(conv_b), row(conv_ln_g), row(conv_ln_b))
    kc, vc = _memkv(mem.reshape(-1, D), row(g_mem), wk_c.astype(BF16), wv_c.astype(BF16))
    n_mem = mem.shape[1]
    h2 = _mix_cross(x, a, c, w_out.astype(BF16), row(g_cross), wq_c.astype(BF16),
                    kc.reshape(B, n_mem, CROSS_WIDTH), vc.reshape(B, n_mem, CROSS_WIDTH),
                    wo_c.astype(BF16))
    out = _mlp(h2.reshape(B * S, D), row(g_mlp), w1.astype(BF16), w2.astype(BF16),
               row(g_final))
    return out.reshape(B, S, D)


def kernel(x, mem, g_mix, w_in, conv_w, conv_b, conv_ln_g, conv_ln_b, w_out, rel_bias,
           g_cross, g_mem, wq_c, wk_c, wv_c, wo_c, g_mlp, w1, w2, g_final):
    return _layer(x, mem, g_mix[0], w_in[0], conv_w[0], conv_b[0], conv_ln_g[0], conv_ln_b[0],
                  w_out[0], rel_bias, g_cross[0], g_mem[0], wq_c[0], wk_c[0], wv_c[0],
                  wo_c[0], g_mlp[0], w1[0], w2[0], g_final)
```
